```python
import jax, jax.numpy as jnp
from jax import lax
import numpy as np

D_MODEL = 2048
BATCH = 4
SEQ = 4096
DEPTH = 2

N_MIXERS = 2
N_A = (DEPTH + 1) // 2
N_B = DEPTH // 2
CHUNK = 128
A_WIDTH = D_MODEL
A_GROUPS = 16
A_GROUP_DIM = A_WIDTH // A_GROUPS
B_HEADS = 16
B_HEAD_DIM = D_MODEL // B_HEADS
Q_BLOCK = 128
FFN_HIDDEN = ((-(-8 * D_MODEL // 3) + 255) // 256) * 256
N_MOD = 6
EPS = 1e-6

kernel_name = "hybrid_sgu_fox_adaln_trunk"


def rms_norm(x, g):
    xf = x.astype(jnp.float32)
    y = xf * lax.rsqrt(jnp.mean(xf * xf, axis=-1, keepdims=True) + EPS)
    return (y * g.astype(jnp.float32)).astype(x.dtype)


def layer_norm(x, g, b):
    xf = x.astype(jnp.float32)
    mu = jnp.mean(xf, axis=-1, keepdims=True)
    var = jnp.mean(jnp.square(xf - mu), axis=-1, keepdims=True)
    y = (xf - mu) * lax.rsqrt(var + EPS)
    return (y * g.astype(jnp.float32) + b.astype(jnp.float32)).astype(x.dtype)


def modulate(h, shift, scale):
    return h * (1 + scale[:, None, :]) + shift[:, None, :]


def chunk_sgu_mixer(h, w_in, b_in, ln_g, ln_b, w_s, b_s, w_out):
    B, S, _ = h.shape
    z = jax.nn.gelu(h @ w_in + b_in, approximate=False)
    u, v = jnp.split(z, 2, axis=-1)
    v = layer_norm(v, ln_g, ln_b)
    causal = jnp.tril(jnp.ones((CHUNK, CHUNK), dtype=bool))
    w = jnp.where(causal[None], w_s, 0).astype(v.dtype)
    vc = v.reshape(B, S // CHUNK, CHUNK, A_GROUPS, A_GROUP_DIM)
    sv = jnp.einsum('gts,bnsgc->bntgc', w, vc) + b_s.T[None, None, :, :, None].astype(v.dtype)
    y = u * sv.reshape(B, S, A_WIDTH)
    return y @ w_out


def forgetting_attention(h, w_in, b_f, w_out):
    B, S, D = h.shape
    proj = h @ w_in
    q, k, v, f_logit = jnp.split(proj, [D, 2 * D, 3 * D], axis=-1)
    q = q.reshape(B, S, B_HEADS, B_HEAD_DIM).transpose(0, 2, 1, 3)
    k = k.reshape(B, S, B_HEADS, B_HEAD_DIM).transpose(0, 2, 1, 3)
    v = v.reshape(B, S, B_HEADS, B_HEAD_DIM).transpose(0, 2, 1, 3)
    log_f = jax.nn.log_sigmoid(f_logit.astype(jnp.float32) + b_f.astype(jnp.float32))
    F = jnp.cumsum(log_f, axis=1).transpose(0, 2, 1)
    n_blk = S // Q_BLOCK
    q_blocks = q.reshape(B, B_HEADS, n_blk, Q_BLOCK, B_HEAD_DIM).transpose(2, 0, 1, 3, 4)
    F_blocks = F.reshape(B, B_HEADS, n_blk, Q_BLOCK).transpose(2, 0, 1, 3)
    k_pos = jnp.arange(S)
    scale = 1.0 / float(np.sqrt(B_HEAD_DIM))

    def attend_block(args):
        q_blk, F_blk, i = args
        q_pos = i * Q_BLOCK + jnp.arange(Q_BLOCK)
        s = jnp.einsum('bhqd,bhkd->bhqk', q_blk, k).astype(jnp.float32) * scale
        s = s + F_blk[..., None] - F[:, :, None, :]
        s = jnp.where(k_pos[None, :] <= q_pos[:, None], s, -jnp.inf)
        p = jax.nn.softmax(s, axis=-1).astype(v.dtype)
        return jnp.einsum('bhqk,bhkd->bhqd', p, v)

    o = lax.map(attend_block, (q_blocks, F_blocks, jnp.arange(n_blk)))
    o = o.transpose(1, 0, 3, 2, 4).reshape(B, S, D)
    return o @ w_out


def swiglu_ffn(h, w_gate, w_up, w_down):
    return (jax.nn.silu(h @ w_gate) * (h @ w_up)) @ w_down


def setup_inputs(seed: int = 0) -> dict:
    key = jax.random.key(seed)
    ks = jax.random.split(key, 24)
    D, F_H, AW = D_MODEL, FFN_HIDDEN, A_WIDTH
    nrm = lambda k, shape, s: jax.random.normal(k, shape, jnp.float32) * s
    x = nrm(ks[0], (BATCH, SEQ, D), 1.0)
    c = nrm(ks[1], (BATCH, D), 1.0)
    ada_w = nrm(ks[2], (DEPTH, D, N_MOD * D), 0.5 * D ** -0.5)
    ada_b = nrm(ks[3], (DEPTH, N_MOD * D), 0.02)
    norm_mix_g = 1.0 + nrm(ks[4], (DEPTH, D), 0.02)
    norm_ffn_g = 1.0 + nrm(ks[5], (DEPTH, D), 0.02)
    a_w_in = nrm(ks[6], (N_A, D, 2 * AW), D ** -0.5)
    a_b_in = nrm(ks[7], (N_A, 2 * AW), 0.02)
    a_ln_g = 1.0 + nrm(ks[8], (N_A, AW), 0.02)
    a_ln_b = nrm(ks[9], (N_A, AW), 0.02)
    a_w_s = nrm(ks[10], (N_A, A_GROUPS, CHUNK, CHUNK), CHUNK ** -0.5)
    a_b_s = 1.0 + nrm(ks[11], (N_A, A_GROUPS, CHUNK), 0.02)
    a_w_out = nrm(ks[12], (N_A, AW, D), AW ** -0.5)
    b_w_qkv = nrm(ks[13], (N_B, D, 3 * D), D ** -0.5)
    b_w_f = nrm(ks[14], (N_B, D, B_HEADS), 0.5 * D ** -0.5)
    b_w_in = jnp.concatenate([b_w_qkv, b_w_f], axis=-1)
    b_b_f = jax.random.uniform(ks[15], (N_B, B_HEADS), jnp.float32, 1.0, 6.0)
    b_w_out = nrm(ks[16], (N_B, D, D), D ** -0.5)
    ffn_w_gate = nrm(ks[17], (DEPTH, D, F_H), D ** -0.5)
    ffn_w_up = nrm(ks[18], (DEPTH, D, F_H), D ** -0.5)
    ffn_w_down = nrm(ks[19], (DEPTH, F_H, D), F_H ** -0.5)
    final_g = 1.0 + nrm(ks[20], (D,), 0.02)
    return {"x": x, "c": c, "ada_w": ada_w, "ada_b": ada_b,
            "norm_mix_g": norm_mix_g, "norm_ffn_g": norm_ffn_g,
            "a_w_in": a_w_in, "a_b_in": a_b_in, "a_ln_g": a_ln_g, "a_ln_b": a_ln_b,
            "a_w_s": a_w_s, "a_b_s": a_b_s, "a_w_out": a_w_out,
            "b_w_in": b_w_in, "b_b_f": b_b_f, "b_w_out": b_w_out,
            "ffn_w_gate": ffn_w_gate, "ffn_w_up": ffn_w_up, "ffn_w_down": ffn_w_down,
            "final_g": final_g}


def reference(x, c, ada_w, ada_b, norm_mix_g, norm_ffn_g,
              a_w_in, a_b_in, a_ln_g, a_ln_b, a_w_s, a_b_s, a_w_out,
              b_w_in, b_b_f, b_w_out,
              ffn_w_gate, ffn_w_up, ffn_w_down, final_g):
    c_act = jax.nn.silu(c)
    for i in range(DEPTH):
        mod = c_act @ ada_w[i] + ada_b[i]
        sh1, sc1, g1, sh2, sc2, g2 = jnp.split(mod, N_MOD, axis=-1)
        h = modulate(rms_norm(x, norm_mix_g[i]), sh1, sc1)
        j = i // N_MIXERS
        if i % N_MIXERS == 0:
            y = chunk_sgu_mixer(h, a_w_in[j], a_b_in[j], a_ln_g[j], a_ln_b[j],
                                a_w_s[j], a_b_s[j], a_w_out[j])
        else:
            y = forgetting_attention(h, b_w_in[j], b_b_f[j], b_w_out[j])
        x = x + g1[:, None, :] * y
        h = modulate(rms_norm(x, norm_ffn_g[i]), sh2, sc2)
        x = x + g2[:, None, :] * swiglu_ffn(h, ffn_w_gate[i], ffn_w_up[i], ffn_w_down[i])
    return rms_norm(x, final_g)
```

```python
import functools
import math

import jax
import jax.numpy as jnp
from jax import lax
from jax.experimental import pallas as pl
from jax.experimental.pallas import tpu as pltpu

F32 = jnp.float32
BF16 = jnp.bfloat16

EPS = 1e-6
CHUNK = 128
GROUPS = 16
HEADS = 16
HEAD_DIM = 128
LANES = 128
VMEM_LIMIT = 56 * 1024 * 1024


def _cparams(*sem):
    return pltpu.CompilerParams(dimension_semantics=sem, vmem_limit_bytes=VMEM_LIMIT)


def _norm_mod(x, g, sh, sc):
    ms = jnp.mean(x * x, axis=-1, keepdims=True)
    y = x * lax.rsqrt(ms + EPS) * g
    return y * (1.0 + sc) + sh


def _ada_kernel(ct_ref, w_ref, b_ref, o_ref):
    ct = ct_ref[...]
    ca = ct * jax.nn.sigmoid(ct)
    w = w_ref[...]
    rows = []
    for b in range(ct.shape[1]):
        rows.append(jnp.sum(w * ca[:, b:b + 1], axis=0, keepdims=True))
    o_ref[...] = jnp.concatenate(rows, axis=0) + b_ref[...]


def _ada_mod(c, ada_w, ada_b, tn=512):
    depth, d, n = ada_w.shape
    bsz = c.shape[0]
    return pl.pallas_call(
        _ada_kernel,
        grid=(depth, n // tn),
        in_specs=[
            pl.BlockSpec((d, bsz), lambda l, j: (0, 0)),
            pl.BlockSpec((None, d, tn), lambda l, j: (l, 0, j)),
            pl.BlockSpec((None, 1, tn), lambda l, j: (l, 0, j)),
        ],
        out_specs=pl.BlockSpec((None, bsz, tn), lambda l, j: (l, 0, j)),
        out_shape=jax.ShapeDtypeStruct((depth, bsz, n), F32),
        compiler_params=_cparams("parallel", "parallel"),
        name="adaln_mod",
    )(c.T, ada_w, ada_b.reshape(depth, 1, n))


def _sgu_in_kernel(x_ref, g_ref, sh_ref, sc_ref, w_ref, b_ref, o_ref, h_ref):
    @pl.when(pl.program_id(1) == 0)
    def _():
        h_ref[...] = _norm_mod(x_ref[...], g_ref[...], sh_ref[...], sc_ref[...]).astype(BF16)

    z = jnp.dot(h_ref[...], w_ref[...], preferred_element_type=F32) + b_ref[...]
    z = 0.5 * z * (1.0 + lax.erf(z * (1.0 / math.sqrt(2.0))))
    o_ref[...] = z.astype(o_ref.dtype)


def _sgu_in(x2, g, sh, sc, w, b, seq, tm=512, tn=1024):
    n, d = x2.shape
    nout = w.shape[1]
    tpb = seq // tm
    return pl.pallas_call(
        _sgu_in_kernel,
        grid=(n // tm, nout // tn),
        in_specs=[
            pl.BlockSpec((tm, d), lambda i, j: (i, 0)),
            pl.BlockSpec((1, d), lambda i, j: (0, 0)),
            pl.BlockSpec((None, 1, d), lambda i, j: (i // tpb, 0, 0)),
            pl.BlockSpec((None, 1, d), lambda i, j: (i // tpb, 0, 0)),
            pl.BlockSpec((d, tn), lambda i, j: (0, j)),
            pl.BlockSpec((1, tn), lambda i, j: (0, j)),
        ],
        out_specs=pl.BlockSpec((tm, tn), lambda i, j: (i, j)),
        out_shape=jax.ShapeDtypeStruct((n, nout), BF16),
        scratch_shapes=[pltpu.VMEM((tm, d), BF16)],
        compiler_params=_cparams("parallel", "arbitrary"),
        name="sgu_in_proj",
    )(x2, g, sh, sc, w, b)


def _sgu_out_kernel(u_ref, v_ref, lg_ref, lb_ref, ws_ref, bst_ref, w_ref, x_ref, gate_ref,
                    o_ref, y_ref):
    tm, aw = u_ref.shape

    @pl.when(pl.program_id(1) == 0)
    def _():
        v = v_ref[...].astype(F32)
        mu = jnp.mean(v, axis=-1, keepdims=True)
        vc = v - mu
        var = jnp.mean(vc * vc, axis=-1, keepdims=True)
        vn = (vc * lax.rsqrt(var + EPS) * lg_ref[...] + lb_ref[...]).astype(BF16)
        row = lax.broadcasted_iota(jnp.int32, (CHUNK, CHUNK), 0)
        col = lax.broadcasted_iota(jnp.int32, (CHUNK, CHUNK), 1)
        causal = col <= row
        for g in range(GROUPS):
            wg = jnp.where(causal, ws_ref[g], 0.0).astype(BF16)
            bcol = bst_ref[:, g:g + 1]
            cs = slice(g * CHUNK, (g + 1) * CHUNK)
            for c in range(tm // CHUNK):
                rs = slice(c * CHUNK, (c + 1) * CHUNK)
                sv = jnp.dot(wg, vn[rs, cs], preferred_element_type=F32) + bcol
                y_ref[rs, cs] = (u_ref[rs, cs].astype(F32) * sv).astype(BF16)

    y = jnp.dot(y_ref[...], w_ref[...], preferred_element_type=F32)
    o_ref[...] = x_ref[...] + gate_ref[...] * y


def _sgu_out(z, ln_g, ln_b, w_s, b_s_t, w_out, x2, gate, seq, tm=512, tn=512):
    n, d = x2.shape
    aw = w_out.shape[0]
    tpb = seq // tm
    return pl.pallas_call(
        _sgu_out_kernel,
        grid=(n // tm, d // tn),
        in_specs=[
            pl.BlockSpec((tm, aw), lambda i, j: (i, 0)),
            pl.BlockSpec((tm, aw), lambda i, j: (i, 1)),
            pl.BlockSpec((1, aw), lambda i, j: (0, 0)),
            pl.BlockSpec((1, aw), lambda i, j: (0, 0)),
            pl.BlockSpec((GROUPS, CHUNK, CHUNK), lambda i, j: (0, 0, 0)),
            pl.BlockSpec((CHUNK, GROUPS), lambda i, j: (0, 0)),
            pl.BlockSpec((aw, tn), lambda i, j: (0, j)),
            pl.BlockSpec((tm, tn), lambda i, j: (i, j)),
            pl.BlockSpec((None, 1, tn), lambda i, j: (i // tpb, 0, j)),
        ],
        out_specs=pl.BlockSpec((tm, tn), lambda i, j: (i, j)),
        out_shape=jax.ShapeDtypeStruct((n, d), F32),
        scratch_shapes=[pltpu.VMEM((tm, aw), BF16)],
        compiler_params=_cparams("parallel", "arbitrary"),
        name="sgu_mix_out_proj",
    )(z, z, ln_g, ln_b, w_s, b_s_t, w_out, x2, gate)


def _ffn_kernel(x_ref, g_ref, sh_ref, sc_ref, gate_ref, wg_ref, wu_ref, wd_ref, fg_ref,
                o_ref, h_ref, acc_ref, *, final_norm):
    f = pl.program_id(1)

    @pl.when(f == 0)
    def _():
        h_ref[...] = _norm_mod(x_ref[...], g_ref[...], sh_ref[...], sc_ref[...]).astype(BF16)
        acc_ref[...] = jnp.zeros_like(acc_ref)

    h = h_ref[...]
    a = jnp.dot(h, wg_ref[...], preferred_element_type=F32)
    b = jnp.dot(h, wu_ref[...], preferred_element_type=F32)
    hid = (a * jax.nn.sigmoid(a) * b).astype(BF16)
    acc_ref[...] += jnp.dot(hid, wd_ref[...], preferred_element_type=F32)

    @pl.when(f == pl.num_programs(1) - 1)
    def _():
        xn = x_ref[...] + gate_ref[...] * acc_ref[...]
        if final_norm:
            ms = jnp.mean(xn * xn, axis=-1, keepdims=True)
            xn = xn * lax.rsqrt(ms + EPS) * fg_ref[...]
        o_ref[...] = xn


def _ffn(x2, g, sh, sc, gate, wg, wu, wd, final_g, seq, final_norm, tm=512, tf=512):
    n, d = x2.shape
    fh = wg.shape[1]
    tpb = seq // tm
    vec = pl.BlockSpec((None, 1, d), lambda i, f: (i // tpb, 0, 0))
    return pl.pallas_call(
        functools.partial(_ffn_kernel, final_norm=final_norm),
        grid=(n // tm, fh // tf),
        in_specs=[
            pl.BlockSpec((tm, d), lambda i, f: (i, 0)),
            pl.BlockSpec((1, d), lambda i, f: (0, 0)),
            vec, vec, vec,
            pl.BlockSpec((d, tf), lambda i, f: (0, f)),
            pl.BlockSpec((d, tf), lambda i, f: (0, f)),
            pl.BlockSpec((tf, d), lambda i, f: (f, 0)),
            pl.BlockSpec((1, d), lambda i, f: (0, 0)),
        ],
        out_specs=pl.BlockSpec((tm, d), lambda i, f: (i, 0)),
        out_shape=jax.ShapeDtypeStruct((n, d), F32),
        scratch_shapes=[pltpu.VMEM((tm, d), BF16), pltpu.VMEM((tm, d), F32)],
        compiler_params=_cparams("parallel", "arbitrary"),
        name="swiglu_ffn",
    )(x2, g, sh, sc, gate, wg, wu, wd, final_g)


def _attn_in_kernel(x_ref, g_ref, sh_ref, sc_ref, w_ref, wf_ref, bf_ref, o_ref, fc_ref,
                    h_ref, carry_ref, *, tiles_per_seq, q_tiles, q_scale):
    i = pl.program_id(0)
    j = pl.program_id(1)
    tm = x_ref.shape[0]

    @pl.when(j == 0)
    def _():
        h = _norm_mod(x_ref[...], g_ref[...], sh_ref[...], sc_ref[...]).astype(BF16)
        h_ref[...] = h
        t = jnp.dot(h, wf_ref[...], preferred_element_type=F32) + bf_ref[...]
        log_f = jnp.minimum(t, 0.0) - jnp.log1p(jnp.exp(-jnp.abs(t)))
        row = lax.broadcasted_iota(jnp.int32, (tm, tm), 0)
        col = lax.broadcasted_iota(jnp.int32, (tm, tm), 1)
        tri = (col <= row).astype(F32)
        cum = jnp.dot(tri, log_f, preferred_element_type=F32, precision=lax.Precision.HIGHEST)

        @pl.when(i % tiles_per_seq == 0)
        def _():
            carry_ref[...] = jnp.zeros_like(carry_ref)

        cum = cum + carry_ref[...]
        fc_ref[...] = cum
        carry_ref[...] = cum[tm - 1:tm, :]

    z = jnp.dot(h_ref[...], w_ref[...], preferred_element_type=F32)
    z = z * jnp.where(j < q_tiles, q_scale, 1.0)
    o_ref[...] = z.astype(o_ref.dtype)


def _attn_in(x2, g, sh, sc, w_qkv, w_f, b_f, seq, tm=512, tn=1024):
    n, d = x2.shape
    nout = w_qkv.shape[1]
    tpb = seq // tm
    kern = functools.partial(_attn_in_kernel, tiles_per_seq=tpb, q_tiles=d // tn,
                             q_scale=1.0 / math.sqrt(HEAD_DIM))
    return pl.pallas_call(
        kern,
        grid=(n // tm, nout // tn),
        in_specs=[
            pl.BlockSpec((tm, d), lambda i, j: (i, 0)),
            pl.BlockSpec((1, d), lambda i, j: (0, 0)),
            pl.BlockSpec((None, 1, d), lambda i, j: (i // tpb, 0, 0)),
            pl.BlockSpec((None, 1, d), lambda i, j: (i // tpb, 0, 0)),
            pl.BlockSpec((d, tn), lambda i, j: (0, j)),
            pl.BlockSpec((d, LANES), lambda i, j: (0, 0)),
            pl.BlockSpec((1, LANES), lambda i, j: (0, 0)),
        ],
        out_specs=[
            pl.BlockSpec((tm, tn), lambda i, j: (i, j)),
            pl.BlockSpec((tm, LANES), lambda i, j: (i, 0)),
        ],
        out_shape=[
            jax.ShapeDtypeStruct((n, nout), BF16),
            jax.ShapeDtypeStruct((n, LANES), F32),
        ],
        scratch_shapes=[pltpu.VMEM((tm, d), BF16), pltpu.VMEM((1, LANES), F32)],
        compiler_params=_cparams("arbitrary", "arbitrary"),
        name="attn_in_proj",
    )(x2, g, sh, sc, w_qkv, w_f, b_f)


def _fox_kernel(q_ref, k_ref, v_ref, fq_ref, fk_ref, o_ref, *, tq):
    h = pl.program_id(1)
    qi = pl.program_id(2)
    q = q_ref[...]
    lane = lax.broadcasted_iota(jnp.int32, fq_ref.shape, 1)
    fq = jnp.sum(jnp.where(lane == h, fq_ref[...], 0.0), axis=1, keepdims=True)

    def step(j, carry, masked):
        m, l, acc = carry
        start = pl.multiple_of(j * tq, tq)
        ks = k_ref[pl.ds(start, tq), :]
        vs = v_ref[pl.ds(start, tq), :]
        fk = fk_ref[pl.ds(j, 1), :]
        s = lax.dot_general(q, ks, (((1,), (1,)), ((), ())), preferred_element_type=F32)
        s = s + (fq - fk)
        if masked:
            row = lax.broadcasted_iota(jnp.int32, s.shape, 0)
            col = lax.broadcasted_iota(jnp.int32, s.shape, 1)
            s = jnp.where(col <= row, s, -jnp.inf)
        m_new = jnp.maximum(m, jnp.max(s, axis=1, keepdims=True))
        p = jnp.exp(s - m_new)
        alpha = jnp.exp(m - m_new)
        l = alpha * l + jnp.sum(p, axis=1, keepdims=True)
        acc = alpha * acc + jnp.dot(p.astype(BF16), vs, preferred_element_type=F32)
        return m_new, l, acc

    init = (jnp.full((tq, 1), -jnp.inf, F32), jnp.zeros((tq, 1), F32),
            jnp.zeros((tq, HEAD_DIM), F32))
    carry = lax.fori_loop(0, qi, functools.partial(step, masked=False), init)
    m, l, acc = step(qi, carry, masked=True)
    o_ref[...] = (acc / l).astype(o_ref.dtype)


def _fox_attention(qkv, fcum, frow, bsz, seq, tq=512):
    n = qkv.shape[0]
    nq = seq // tq
    return pl.pallas_call(
        functools.partial(_fox_kernel, tq=tq),
        grid=(bsz, HEADS, nq),
        in_specs=[
            pl.BlockSpec((tq, HEAD_DIM), lambda b, h, i: (b * nq + i, h)),
            pl.BlockSpec((seq, HEAD_DIM), lambda b, h, i: (b, HEADS + h)),
            pl.BlockSpec((seq, HEAD_DIM), lambda b, h, i: (b, 2 * HEADS + h)),
            pl.BlockSpec((tq, LANES), lambda b, h, i: (b * nq + i, 0)),
            pl.BlockSpec((None, nq, tq), lambda b, h, i: (b * HEADS + h, 0, 0)),
        ],
        out_specs=pl.BlockSpec((tq, HEAD_DIM), lambda b, h, i: (b * nq + i, h)),
        out_shape=jax.ShapeDtypeStruct((n, HEADS * HEAD_DIM), BF16),
        compiler_params=_cparams("parallel", "parallel", "arbitrary"),
        name="fox_attention",
    )(qkv, qkv, qkv, fcum, frow)


def _proj_res_kernel(a_ref, w_ref, x_ref, gate_ref, o_ref):
    y = jnp.dot(a_ref[...], w_ref[...], preferred_element_type=F32)
    o_ref[...] = x_ref[...] + gate_ref[...] * y


def _proj_res(a, w, x2, gate, seq, tm=512, tn=1024):
    n, d = x2.shape
    k = a.shape[1]
    tpb = seq // tm
    return pl.pallas_call(
        _proj_res_kernel,
        grid=(n // tm, d // tn),
        in_specs=[
            pl.BlockSpec((tm, k), lambda i, j: (i, 0)),
            pl.BlockSpec((k, tn), lambda i, j: (0, j)),
            pl.BlockSpec((tm, tn), lambda i, j: (i, j)),
            pl.BlockSpec((None, 1, tn), lambda i, j: (i // tpb, 0, j)),
        ],
        out_specs=pl.BlockSpec((tm, tn), lambda i, j: (i, j)),
        out_shape=jax.ShapeDtypeStruct((n, d), F32),
        compiler_params=_cparams("parallel", "arbitrary"),
        name="attn_out_proj",
    )(a, w, x2, gate)


def kernel(x, c, ada_w, ada_b, norm_mix_g, norm_ffn_g, a_w_in, a_b_in, a_ln_g, a_ln_b, a_w_s,
           a_b_s, a_w_out, b_w_in, b_b_f, b_w_out, ffn_w_gate, ffn_w_up, ffn_w_down, final_g):
    bsz, seq, d = x.shape
    depth = ada_w.shape[0]
    x2 = x.reshape(bsz * seq, d)

    mod = _ada_mod(c, ada_w, ada_b).reshape(depth, bsz, 6, 1, d)
    row = lambda a: a.reshape(1, -1)

    for i in range(depth):
        sh1, sc1, g1, sh2, sc2, g2 = (mod[i, :, k] for k in range(6))
        j = i // 2
        if i % 2 == 0:
            z = _sgu_in(x2, row(norm_mix_g[i]), sh1, sc1, a_w_in[j].astype(BF16),
                        row(a_b_in[j]), seq)
            x2 = _sgu_out(z, row(a_ln_g[j]), row(a_ln_b[j]), a_w_s[j], a_b_s[j].T,
                          a_w_out[j].astype(BF16), x2, g1, seq)
        else:
            w_in = b_w_in[j]
            w_qkv = w_in[:, :3 * d].astype(BF16)
            w_f = jnp.pad(w_in[:, 3 * d:], ((0, 0), (0, LANES - HEADS))).astype(BF16)
            b_f = jnp.pad(b_b_f[j], (0, LANES - HEADS)).reshape(1, LANES)
            qkv, fcum = _attn_in(x2, row(norm_mix_g[i]), sh1, sc1, w_qkv, w_f, b_f, seq)
            tq = 512
            frow = fcum[:, :HEADS].reshape(bsz, seq, HEADS).transpose(0, 2, 1)
            frow = frow.reshape(bsz * HEADS, seq // tq, tq)
            o = _fox_attention(qkv, fcum, frow, bsz, seq, tq=tq)
            x2 = _proj_res(o, b_w_out[j].astype(BF16), x2, g1, seq)
        x2 = _ffn(x2, row(norm_ffn_g[i]), sh2, sc2, g2, ffn_w_gate[i].astype(BF16),
                  ffn_w_up[i].astype(BF16), ffn_w_down[i].astype(BF16), row(final_g), seq,
                  final_norm=(i == depth - 1))
    return x2.reshape(bsz, seq, d)
```

```python
import functools
import math

import jax
import jax.numpy as jnp
from jax import lax
from jax.experimental import pallas as pl
from jax.experimental.pallas import tpu as pltpu

F32 = jnp.float32
BF16 = jnp.bfloat16

EPS = 1e-6
CHUNK = 128
GROUPS = 16
HEADS = 16
HEAD_DIM = 128
LANES = 128
LOG2E = math.log2(math.e)
VMEM_LIMIT = 56 * 1024 * 1024


def _cparams(*sem):
    return pltpu.CompilerParams(dimension_semantics=sem, vmem_limit_bytes=VMEM_LIMIT)


def _row_chunks(tm, rc):
    return [slice(r * rc, (r + 1) * rc) for r in range(tm // rc)]


def _norm_mod(x, gm, sh):
    ms = jnp.mean(x * x, axis=-1, keepdims=True)
    return x * lax.rsqrt(ms + EPS) * gm + sh


def _mm(a, b):
    return jnp.dot(a, b, preferred_element_type=F32)


def _ada_kernel(ct_ref, w_ref, b_ref, o_ref):
    ct = ct_ref[...]
    ca = ct * jax.nn.sigmoid(ct)
    w = w_ref[...]
    rows = []
    for b in range(ct.shape[1]):
        rows.append(jnp.sum(w * ca[:, b:b + 1], axis=0, keepdims=True))
    o_ref[...] = jnp.concatenate(rows, axis=0) + b_ref[...]


def _ada_mod(c, ada_w, ada_b, tn=512):
    depth, d, n = ada_w.shape
    bsz = c.shape[0]
    return pl.pallas_call(
        _ada_kernel,
        grid=(depth, n // tn),
        in_specs=[
            pl.BlockSpec((d, bsz), lambda l, j: (0, 0)),
            pl.BlockSpec((None, d, tn), lambda l, j: (l, 0, j)),
            pl.BlockSpec((None, 1, tn), lambda l, j: (l, 0, j)),
        ],
        out_specs=pl.BlockSpec((None, bsz, tn), lambda l, j: (l, 0, j)),
        out_shape=jax.ShapeDtypeStruct((depth, bsz, n), F32),
        compiler_params=_cparams("parallel", "parallel"),
        name="adaln_mod",
    )(c.T, ada_w, ada_b.reshape(depth, 1, n))


def _sgu_in_kernel(x_ref, g_ref, sh_ref, sc_ref, w_ref, b_ref, o_ref, h_ref, *, rc):
    j = pl.program_id(1)

    def emit(h, rs):
        z = _mm(h, w_ref[...]) + b_ref[...]
        z = 0.5 * z * (1.0 + lax.erf(z * (1.0 / math.sqrt(2.0))))
        o_ref[rs, :] = z.astype(o_ref.dtype)

    @pl.when(j == 0)
    def _():
        gm = g_ref[...] * (1.0 + sc_ref[...])
        sh = sh_ref[...]
        for rs in _row_chunks(x_ref.shape[0], rc):
            h = _norm_mod(x_ref[rs, :], gm, sh).astype(BF16)
            h_ref[rs, :] = h
            emit(h, rs)

    @pl.when(j != 0)
    def _():
        emit(h_ref[...], slice(None))


def _sgu_in(x2, g, sh, sc, w, b, seq, tm=1024, tn=1024, rc=256):
    n, d = x2.shape
    nout = w.shape[1]
    tpb = seq // tm
    return pl.pallas_call(
        functools.partial(_sgu_in_kernel, rc=rc),
        grid=(n // tm, nout // tn),
        in_specs=[
            pl.BlockSpec((tm, d), lambda i, j: (i, 0)),
            pl.BlockSpec((1, d), lambda i, j: (0, 0)),
            pl.BlockSpec((None, 1, d), lambda i, j: (i // tpb, 0, 0)),
            pl.BlockSpec((None, 1, d), lambda i, j: (i // tpb, 0, 0)),
            pl.BlockSpec((d, tn), lambda i, j: (0, j)),
            pl.BlockSpec((1, tn), lambda i, j: (0, j)),
        ],
        out_specs=pl.BlockSpec((tm, tn), lambda i, j: (i, j)),
        out_shape=jax.ShapeDtypeStruct((n, nout), BF16),
        scratch_shapes=[pltpu.VMEM((tm, d), BF16)],
        compiler_params=_cparams("parallel", "arbitrary"),
        name="sgu_in_proj",
    )(x2, g, sh, sc, w, b)


def _sgu_out_kernel(u_ref, v_ref, lg_ref, lb_ref, ws_ref, bst_ref, w_ref, x_ref, gate_ref,
                    o_ref, y_ref, wm_ref, *, rc):
    tm, aw = u_ref.shape
    j = pl.program_id(1)
    nck = rc // CHUNK

    def emit(y, rs):
        o_ref[rs, :] = x_ref[rs, :] + gate_ref[...] * _mm(y, w_ref[...])

    @pl.when(j == 0)
    def _():
        row = lax.broadcasted_iota(jnp.int32, (CHUNK, CHUNK), 0)
        col = lax.broadcasted_iota(jnp.int32, (CHUNK, CHUNK), 1)
        causal = col <= row
        for g in range(GROUPS):
            wm_ref[g] = jnp.where(causal, ws_ref[g], 0.0).astype(BF16)
        lg = lg_ref[...]
        lb = lb_ref[...]
        for rs in _row_chunks(tm, rc):
            v = v_ref[rs, :].astype(F32)
            mu = jnp.mean(v, axis=-1, keepdims=True)
            vc = v - mu
            var = jnp.mean(vc * vc, axis=-1, keepdims=True)
            vn = (vc * lax.rsqrt(var + EPS) * lg + lb).astype(BF16)
            u = u_ref[rs, :]
            for g in range(GROUPS):
                cs = slice(g * CHUNK, (g + 1) * CHUNK)
                vg = jnp.concatenate(
                    [vn[c * CHUNK:(c + 1) * CHUNK, cs] for c in range(nck)], axis=1)
                sv = _mm(wm_ref[g], vg) + bst_ref[:, g:g + 1]
                for c in range(nck):
                    r0 = rs.start + c * CHUNK
                    ug = u[c * CHUNK:(c + 1) * CHUNK, cs].astype(F32)
                    y_ref[r0:r0 + CHUNK, cs] = (
                        ug * sv[:, c * CHUNK:(c + 1) * CHUNK]).astype(BF16)
            emit(y_ref[rs, :], rs)

    @pl.when(j != 0)
    def _():
        emit(y_ref[...], slice(None))


def _sgu_out(z, ln_g, ln_b, w_s, b_s_t, w_out, x2, gate, seq, tm=1024, tn=512, rc=256):
    n, d = x2.shape
    aw = w_out.shape[0]
    tpb = seq // tm
    return pl.pallas_call(
        functools.partial(_sgu_out_kernel, rc=rc),
        grid=(n // tm, d // tn),
        in_specs=[
            pl.BlockSpec((tm, aw), lambda i, j: (i, 0)),
            pl.BlockSpec((tm, aw), lambda i, j: (i, 1)),
            pl.BlockSpec((1, aw), lambda i, j: (0, 0)),
            pl.BlockSpec((1, aw), lambda i, j: (0, 0)),
            pl.BlockSpec((GROUPS, CHUNK, CHUNK), lambda i, j: (0, 0, 0)),
            pl.BlockSpec((CHUNK, GROUPS), lambda i, j: (0, 0)),
            pl.BlockSpec((aw, tn), lambda i, j: (0, j)),
            pl.BlockSpec((tm, tn), lambda i, j: (i, j)),
            pl.BlockSpec((None, 1, tn), lambda i, j: (i // tpb, 0, j)),
        ],
        out_specs=pl.BlockSpec((tm, tn), lambda i, j: (i, j)),
        out_shape=jax.ShapeDtypeStruct((n, d), F32),
        scratch_shapes=[pltpu.VMEM((tm, aw), BF16), pltpu.VMEM((GROUPS, CHUNK, CHUNK), BF16)],
        compiler_params=_cparams("parallel", "arbitrary"),
        name="sgu_mix_out_proj",
    )(z, z, ln_g, ln_b, w_s, b_s_t, w_out, x2, gate)


def _ffn_kernel(x_ref, g_ref, sh_ref, sc_ref, gate_ref, wg_ref, wu_ref, wd_ref, fg_ref,
                o_ref, h_ref, *, final_norm, rc_edge, rc_mid):
    f = pl.program_id(1)
    last = pl.num_programs(1) - 1
    tm = x_ref.shape[0]

    def mlp(h):
        a = _mm(h, wg_ref[...])
        b = _mm(h, wu_ref[...])
        hid = (a * jax.nn.sigmoid(a) * b).astype(BF16)
        return gate_ref[...] * _mm(hid, wd_ref[...])

    @pl.when(f == 0)
    def _():
        gm = g_ref[...] * (1.0 + sc_ref[...])
        sh = sh_ref[...]
        for rs in _row_chunks(tm, rc_edge):
            x = x_ref[rs, :]
            h = _norm_mod(x, gm, sh).astype(BF16)
            h_ref[rs, :] = h
            o_ref[rs, :] = x + mlp(h)

    def accumulate(rc, normalize):
        for rs in _row_chunks(tm, rc):
            o = o_ref[rs, :] + mlp(h_ref[rs, :])
            if normalize:
                ms = jnp.mean(o * o, axis=-1, keepdims=True)
                o = o * lax.rsqrt(ms + EPS) * fg_ref[...]
            o_ref[rs, :] = o

    if final_norm:
        @pl.when((f > 0) & (f < last))
        def _():
            accumulate(rc_mid, False)

        @pl.when(f == last)
        def _():
            accumulate(rc_edge, True)
    else:
        @pl.when(f > 0)
        def _():
            accumulate(rc_mid, False)


def _ffn(x2, g, sh, sc, gate, wg, wu, wd, final_g, seq, final_norm, tm=1024, tf=256,
         rc_edge=256, rc_mid=512):
    n, d = x2.shape
    fh = wg.shape[1]
    tpb = seq // tm
    vec = pl.BlockSpec((None, 1, d), lambda i, f: (i // tpb, 0, 0))
    return pl.pallas_call(
        functools.partial(_ffn_kernel, final_norm=final_norm, rc_edge=rc_edge, rc_mid=rc_mid),
        grid=(n // tm, fh // tf),
        in_specs=[
            pl.BlockSpec((tm, d), lambda i, f: (i, 0)),
            pl.BlockSpec((1, d), lambda i, f: (0, 0)),
            vec, vec, vec,
            pl.BlockSpec((d, tf), lambda i, f: (0, f)),
            pl.BlockSpec((d, tf), lambda i, f: (0, f)),
            pl.BlockSpec((tf, d), lambda i, f: (f, 0)),
            pl.BlockSpec((1, d), lambda i, f: (0, 0)),
        ],
        out_specs=pl.BlockSpec((tm, d), lambda i, f: (i, 0)),
        out_shape=jax.ShapeDtypeStruct((n, d), F32),
        scratch_shapes=[pltpu.VMEM((tm, d), BF16)],
        compiler_params=_cparams("parallel", "arbitrary"),
        name="swiglu_ffn",
    )(x2, g, sh, sc, gate, wg, wu, wd, final_g)


def _attn_in_kernel(x_ref, g_ref, sh_ref, sc_ref, w_ref, wf_ref, bf_ref, o_ref, fc_ref,
                    h_ref, carry_ref, *, tiles_per_seq, q_tiles, q_scale, rc):
    i = pl.program_id(0)
    j = pl.program_id(1)

    def emit(h, rs):
        z = _mm(h, w_ref[...]) * jnp.where(j < q_tiles, q_scale, 1.0)
        o_ref[rs, :] = z.astype(o_ref.dtype)

    @pl.when(j == 0)
    def _():
        @pl.when(i % tiles_per_seq == 0)
        def _():
            carry_ref[...] = jnp.zeros_like(carry_ref)

        gm = g_ref[...] * (1.0 + sc_ref[...])
        sh = sh_ref[...]
        row = lax.broadcasted_iota(jnp.int32, (rc, rc), 0)
        col = lax.broadcasted_iota(jnp.int32, (rc, rc), 1)
        tri = (col <= row).astype(F32)
        for rs in _row_chunks(x_ref.shape[0], rc):
            h = _norm_mod(x_ref[rs, :], gm, sh).astype(BF16)
            h_ref[rs, :] = h
            t = _mm(h, wf_ref[...]) + bf_ref[...]
            log_f = jnp.minimum(t, 0.0) - jnp.log1p(jnp.exp(-jnp.abs(t)))
            cum = jnp.dot(tri, log_f, preferred_element_type=F32,
                          precision=lax.Precision.HIGHEST) + carry_ref[...]
            fc_ref[rs, :] = cum * LOG2E
            carry_ref[...] = cum[rc - 1:rc, :]
            emit(h, rs)

    @pl.when(j != 0)
    def _():
        emit(h_ref[...], slice(None))


def _attn_in(x2, g, sh, sc, w_qkv, w_f, b_f, seq, tm=1024, tn=1024, rc=256):
    n, d = x2.shape
    nout = w_qkv.shape[1]
    tpb = seq // tm
    kern = functools.partial(_attn_in_kernel, tiles_per_seq=tpb, q_tiles=d // tn,
                             q_scale=LOG2E / math.sqrt(HEAD_DIM), rc=rc)
    return pl.pallas_call(
        kern,
        grid=(n // tm, nout // tn),
        in_specs=[
            pl.BlockSpec((tm, d), lambda i, j: (i, 0)),
            pl.BlockSpec((1, d), lambda i, j: (0, 0)),
            pl.BlockSpec((None, 1, d), lambda i, j: (i // tpb, 0, 0)),
            pl.BlockSpec((None, 1, d), lambda i, j: (i // tpb, 0, 0)),
            pl.BlockSpec((d, tn), lambda i, j: (0, j)),
            pl.BlockSpec((d, LANES), lambda i, j: (0, 0)),
            pl.BlockSpec((1, LANES), lambda i, j: (0, 0)),
        ],
        out_specs=[
            pl.BlockSpec((tm, tn), lambda i, j: (i, j)),
            pl.BlockSpec((tm, LANES), lambda i, j: (i, 0)),
        ],
        out_shape=[
            jax.ShapeDtypeStruct((n, nout), BF16),
            jax.ShapeDtypeStruct((n, LANES), F32),
        ],
        scratch_shapes=[pltpu.VMEM((tm, d), BF16), pltpu.VMEM((1, LANES), F32)],
        compiler_params=_cparams("arbitrary", "arbitrary"),
        name="attn_in_proj",
    )(x2, g, sh, sc, w_qkv, w_f, b_f)


def _fox_kernel(q_ref, k_ref, v_ref, fq_ref, fk_ref, o_ref, va_ref, *, tq, hb, nsplit):
    hg = pl.program_id(1)
    qi = pl.program_id(2)
    rows = tq // nsplit
    dh = HEAD_DIM

    @pl.when(qi == 0)
    def _():
        for hh in range(hb):
            va_ref[hh, :, :dh] = v_ref[:, hh * dh:(hh + 1) * dh]
            va_ref[hh, :, dh:] = jnp.ones((va_ref.shape[1], dh), BF16)

    lane = lax.broadcasted_iota(jnp.int32, fq_ref.shape, 1)
    fq_tile = fq_ref[...]
    fq_heads = [jnp.sum(jnp.where(lane == hg * hb + hh, fq_tile, 0.0), axis=1, keepdims=True)
                for hh in range(hb)]
    chains = [(hh, hf) for hh in range(hb) for hf in range(nsplit)]

    def step(j, carry, masked):
        start = pl.multiple_of(j * tq, tq)
        out = []
        for (hh, hf), (m, acc) in zip(chains, carry):
            ncols = (hf + 1) * rows if masked else tq
            ks = k_ref[pl.ds(start, ncols), hh * dh:(hh + 1) * dh]
            va = va_ref[hh, pl.ds(start, ncols), :]
            fk = fk_ref[hh, pl.ds(j, 1), :][:, :ncols]
            q = q_ref[hf * rows:(hf + 1) * rows, hh * dh:(hh + 1) * dh]
            fq = fq_heads[hh][hf * rows:(hf + 1) * rows, :]
            t = lax.dot_general(q, ks, (((1,), (1,)), ((), ())), preferred_element_type=F32) - fk
            if masked:
                row = lax.broadcasted_iota(jnp.int32, t.shape, 0) + hf * rows
                col = lax.broadcasted_iota(jnp.int32, t.shape, 1)
                t = jnp.where(col <= row, t, -jnp.inf)
            m_new = jnp.maximum(m, jnp.max(t, axis=1, keepdims=True) + fq)
            p = jnp.exp2(t + (fq - m_new))
            alpha = jnp.exp2(m - m_new)
            acc = alpha * acc + _mm(p.astype(BF16), va)
            out.append((m_new, acc))
        return tuple(out)

    init = tuple((jnp.full((rows, 1), -jnp.inf, F32), jnp.zeros((rows, 2 * dh), F32))
                 for _ in chains)
    carry = lax.fori_loop(0, qi, functools.partial(step, masked=False), init)
    carry = step(qi, carry, masked=True)
    for (hh, hf), (_, acc) in zip(chains, carry):
        o_ref[hf * rows:(hf + 1) * rows, hh * dh:(hh + 1) * dh] = (
            acc[:, :dh] / acc[:, dh:]).astype(o_ref.dtype)


def _fox_attention(qkv, fcum, frow, bsz, seq, tq=512, hb=4, nsplit=1):
    n = qkv.shape[0]
    nq = seq // tq
    ng = HEADS // hb
    w = hb * HEAD_DIM
    return pl.pallas_call(
        functools.partial(_fox_kernel, tq=tq, hb=hb, nsplit=nsplit),
        grid=(bsz, ng, nq),
        in_specs=[
            pl.BlockSpec((tq, w), lambda b, g, i: (b * nq + i, g)),
            pl.BlockSpec((seq, w), lambda b, g, i: (b, ng + g)),
            pl.BlockSpec((seq, w), lambda b, g, i: (b, 2 * ng + g)),
            pl.BlockSpec((tq, LANES), lambda b, g, i: (b * nq + i, 0)),
            pl.BlockSpec((hb, nq, tq), lambda b, g, i: (b * ng + g, 0, 0)),
        ],
        out_specs=pl.BlockSpec((tq, w), lambda b, g, i: (b * nq + i, g)),
        out_shape=jax.ShapeDtypeStruct((n, HEADS * HEAD_DIM), BF16),
        scratch_shapes=[pltpu.VMEM((hb, seq, 2 * HEAD_DIM), BF16)],
        compiler_params=_cparams("arbitrary", "arbitrary", "arbitrary"),
        name="fox_attention",
    )(qkv, qkv, qkv, fcum, frow)


def _proj_res_kernel(a_ref, w_ref, x_ref, gate_ref, o_ref):
    o_ref[...] = x_ref[...] + gate_ref[...] * _mm(a_ref[...], w_ref[...])


def _proj_res(a, w, x2, gate, seq, tm=1024, tn=1024):
    n, d = x2.shape
    k = a.shape[1]
    tpb = seq // tm
    return pl.pallas_call(
        _proj_res_kernel,
        grid=(n // tm, d // tn),
        in_specs=[
            pl.BlockSpec((tm, k), lambda i, j: (i, 0)),
            pl.BlockSpec((k, tn), lambda i, j: (0, j)),
            pl.BlockSpec((tm, tn), lambda i, j: (i, j)),
            pl.BlockSpec((None, 1, tn), lambda i, j: (i // tpb, 0, j)),
        ],
        out_specs=pl.BlockSpec((tm, tn), lambda i, j: (i, j)),
        out_shape=jax.ShapeDtypeStruct((n, d), F32),
        compiler_params=_cparams("parallel", "arbitrary"),
        name="attn_out_proj",
    )(a, w, x2, gate)


def kernel(x, c, ada_w, ada_b, norm_mix_g, norm_ffn_g, a_w_in, a_b_in, a_ln_g, a_ln_b, a_w_s,
           a_b_s, a_w_out, b_w_in, b_b_f, b_w_out, ffn_w_gate, ffn_w_up, ffn_w_down, final_g):
    bsz, seq, d = x.shape
    depth = ada_w.shape[0]
    x2 = x.reshape(bsz * seq, d)

    mod = _ada_mod(c, ada_w, ada_b).reshape(depth, bsz, 6, 1, d)
    row = lambda a: a.reshape(1, -1)

    for i in range(depth):
        sh1, sc1, g1, sh2, sc2, g2 = (mod[i, :, k] for k in range(6))
        j = i // 2
        if i % 2 == 0:
            z = _sgu_in(x2, row(norm_mix_g[i]), sh1, sc1, a_w_in[j].astype(BF16),
                        row(a_b_in[j]), seq)
            x2 = _sgu_out(z, row(a_ln_g[j]), row(a_ln_b[j]), a_w_s[j], a_b_s[j].T,
                          a_w_out[j].astype(BF16), x2, g1, seq)
        else:
            w_in = b_w_in[j]
            w_qkv = w_in[:, :3 * d].astype(BF16)
            w_f = jnp.pad(w_in[:, 3 * d:], ((0, 0), (0, LANES - HEADS))).astype(BF16)
            b_f = jnp.pad(b_b_f[j], (0, LANES - HEADS)).reshape(1, LANES)
            qkv, fcum = _attn_in(x2, row(norm_mix_g[i]), sh1, sc1, w_qkv, w_f, b_f, seq)
            tq = 512
            frow = fcum[:, :HEADS].reshape(bsz, seq, HEADS).transpose(0, 2, 1)
            frow = frow.reshape(bsz * HEADS, seq // tq, tq)
            o = _fox_attention(qkv, fcum, frow, bsz, seq, tq=tq)
            x2 = _proj_res(o, b_w_out[j].astype(BF16), x2, g1, seq)
        x2 = _ffn(x2, row(norm_ffn_g[i]), sh2, sc2, g2, ffn_w_gate[i].astype(BF16),
                  ffn_w_up[i].astype(BF16), ffn_w_down[i].astype(BF16), row(final_g), seq,
                  final_norm=(i == depth - 1))
    return x2.reshape(bsz, seq, d)
```

```python
import functools
import math

import jax
import jax.numpy as jnp
from jax import lax
from jax.experimental import pallas as pl
from jax.experimental.pallas import tpu as pltpu

F32 = jnp.float32
BF16 = jnp.bfloat16

EPS = 1e-6
CHUNK = 128
GROUPS = 16
HEADS = 16
HEAD_DIM = 128
LANES = 128
LOG2E = math.log2(math.e)
VMEM_LIMIT = 56 * 1024 * 1024


def _cparams(*sem):
    return pltpu.CompilerParams(dimension_semantics=sem, vmem_limit_bytes=VMEM_LIMIT)


def _row_chunks(tm, rc):
    return [slice(r * rc, (r + 1) * rc) for r in range(tm // rc)]


def _norm_mod(x, gm, sh):
    ms = jnp.mean(x * x, axis=-1, keepdims=True)
    return x * lax.rsqrt(ms + EPS) * gm + sh


def _mm(a, b):
    return jnp.dot(a, b, preferred_element_type=F32)


def _ada_kernel(ct_ref, w_ref, b_ref, o_ref):
    ct = ct_ref[...]
    ca = ct * jax.nn.sigmoid(ct)
    w = w_ref[...]
    rows = []
    for b in range(ct.shape[1]):
        rows.append(jnp.sum(w * ca[:, b:b + 1], axis=0, keepdims=True))
    o_ref[...] = jnp.concatenate(rows, axis=0) + b_ref[...]


def _ada_mod(c, ada_w, ada_b, tn=512):
    depth, d, n = ada_w.shape
    bsz = c.shape[0]
    return pl.pallas_call(
        _ada_kernel,
        grid=(depth, n // tn),
        in_specs=[
            pl.BlockSpec((d, bsz), lambda l, j: (0, 0)),
            pl.BlockSpec((None, d, tn), lambda l, j: (l, 0, j)),
            pl.BlockSpec((None, 1, tn), lambda l, j: (l, 0, j)),
        ],
        out_specs=pl.BlockSpec((None, bsz, tn), lambda l, j: (l, 0, j)),
        out_shape=jax.ShapeDtypeStruct((depth, bsz, n), F32),
        compiler_params=_cparams("parallel", "parallel"),
        name="adaln_mod",
    )(c.T, ada_w, ada_b.reshape(depth, 1, n))


def _sgu_in_kernel(x_ref, g_ref, sh_ref, sc_ref, w_ref, b_ref, o_ref, h_ref, *, rc):
    j = pl.program_id(1)

    def emit(h, rs):
        z = _mm(h, w_ref[...]) + b_ref[...]
        z = 0.5 * z * (1.0 + lax.erf(z * (1.0 / math.sqrt(2.0))))
        o_ref[rs, :] = z.astype(o_ref.dtype)

    @pl.when(j == 0)
    def _():
        gm = g_ref[...] * (1.0 + sc_ref[...])
        sh = sh_ref[...]
        for rs in _row_chunks(x_ref.shape[0], rc):
            h = _norm_mod(x_ref[rs, :], gm, sh).astype(BF16)
            h_ref[rs, :] = h
            emit(h, rs)

    @pl.when(j != 0)
    def _():
        emit(h_ref[...], slice(None))


def _sgu_in(x2, g, sh, sc, w, b, seq, tm=1024, tn=1024, rc=256):
    n, d = x2.shape
    nout = w.shape[1]
    tpb = seq // tm
    return pl.pallas_call(
        functools.partial(_sgu_in_kernel, rc=rc),
        grid=(n // tm, nout // tn),
        in_specs=[
            pl.BlockSpec((tm, d), lambda i, j: (i, 0)),
            pl.BlockSpec((1, d), lambda i, j: (0, 0)),
            pl.BlockSpec((None, 1, d), lambda i, j: (i // tpb, 0, 0)),
            pl.BlockSpec((None, 1, d), lambda i, j: (i // tpb, 0, 0)),
            pl.BlockSpec((d, tn), lambda i, j: (0, j)),
            pl.BlockSpec((1, tn), lambda i, j: (0, j)),
        ],
        out_specs=pl.BlockSpec((tm, tn), lambda i, j: (i, j)),
        out_shape=jax.ShapeDtypeStruct((n, nout), BF16),
        scratch_shapes=[pltpu.VMEM((tm, d), BF16)],
        compiler_params=_cparams("parallel", "arbitrary"),
        name="sgu_in_proj",
    )(x2, g, sh, sc, w, b)


def _sgu_out_kernel(u_ref, v_ref, lg_ref, lb_ref, ws_ref, bst_ref, w_ref, x_ref, gate_ref,
                    o_ref, y_ref, wm_ref, *, rc):
    tm, aw = u_ref.shape
    j = pl.program_id(1)
    nck = rc // CHUNK

    def emit(y, rs):
        o_ref[rs, :] = x_ref[rs, :] + gate_ref[...] * _mm(y, w_ref[...])

    @pl.when(j == 0)
    def _():
        row = lax.broadcasted_iota(jnp.int32, (CHUNK, CHUNK), 0)
        col = lax.broadcasted_iota(jnp.int32, (CHUNK, CHUNK), 1)
        causal = col <= row
        for g in range(GROUPS):
            wm_ref[g] = jnp.where(causal, ws_ref[g], 0.0).astype(BF16)
        lg = lg_ref[...]
        lb = lb_ref[...]
        for rs in _row_chunks(tm, rc):
            v = v_ref[rs, :].astype(F32)
            mu = jnp.mean(v, axis=-1, keepdims=True)
            vc = v - mu
            var = jnp.mean(vc * vc, axis=-1, keepdims=True)
            vn = (vc * lax.rsqrt(var + EPS) * lg + lb).astype(BF16)
            u = u_ref[rs, :]
            for g in range(GROUPS):
                cs = slice(g * CHUNK, (g + 1) * CHUNK)
                vg = jnp.concatenate(
                    [vn[c * CHUNK:(c + 1) * CHUNK, cs] for c in range(nck)], axis=1)
                sv = _mm(wm_ref[g], vg) + bst_ref[:, g:g + 1]
                for c in range(nck):
                    r0 = rs.start + c * CHUNK
                    ug = u[c * CHUNK:(c + 1) * CHUNK, cs].astype(F32)
                    y_ref[r0:r0 + CHUNK, cs] = (
                        ug * sv[:, c * CHUNK:(c + 1) * CHUNK]).astype(BF16)
            emit(y_ref[rs, :], rs)

    @pl.when(j != 0)
    def _():
        emit(y_ref[...], slice(None))


def _sgu_out(z, ln_g, ln_b, w_s, b_s_t, w_out, x2, gate, seq, tm=1024, tn=512, rc=256):
    n, d = x2.shape
    aw = w_out.shape[0]
    tpb = seq // tm
    return pl.pallas_call(
        functools.partial(_sgu_out_kernel, rc=rc),
        grid=(n // tm, d // tn),
        in_specs=[
            pl.BlockSpec((tm, aw), lambda i, j: (i, 0)),
            pl.BlockSpec((tm, aw), lambda i, j: (i, 1)),
            pl.BlockSpec((1, aw), lambda i, j: (0, 0)),
            pl.BlockSpec((1, aw), lambda i, j: (0, 0)),
            pl.BlockSpec((GROUPS, CHUNK, CHUNK), lambda i, j: (0, 0, 0)),
            pl.BlockSpec((CHUNK, GROUPS), lambda i, j: (0, 0)),
            pl.BlockSpec((aw, tn), lambda i, j: (0, j)),
            pl.BlockSpec((tm, tn), lambda i, j: (i, j)),
            pl.BlockSpec((None, 1, tn), lambda i, j: (i // tpb, 0, j)),
        ],
        out_specs=pl.BlockSpec((tm, tn), lambda i, j: (i, j)),
        out_shape=jax.ShapeDtypeStruct((n, d), F32),
        scratch_shapes=[pltpu.VMEM((tm, aw), BF16), pltpu.VMEM((GROUPS, CHUNK, CHUNK), BF16)],
        compiler_params=_cparams("parallel", "arbitrary"),
        name="sgu_mix_out_proj",
    )(z, z, ln_g, ln_b, w_s, b_s_t, w_out, x2, gate)


def _ffn_kernel(x_ref, g_ref, sh_ref, sc_ref, gate_ref, wg_ref, wu_ref, wd_ref, fg_ref,
                o_ref, h_ref, *, final_norm, rc_edge, rc_mid):
    f = pl.program_id(1)
    last = pl.num_programs(1) - 1
    tm = x_ref.shape[0]

    def mlp(h):
        a = _mm(h, wg_ref[...])
        b = _mm(h, wu_ref[...])
        hid = (a * jax.nn.sigmoid(a) * b).astype(BF16)
        return gate_ref[...] * _mm(hid, wd_ref[...])

    @pl.when(f == 0)
    def _():
        gm = g_ref[...] * (1.0 + sc_ref[...])
        sh = sh_ref[...]
        for rs in _row_chunks(tm, rc_edge):
            x = x_ref[rs, :]
            h = _norm_mod(x, gm, sh).astype(BF16)
            h_ref[rs, :] = h
            o_ref[rs, :] = x + mlp(h)

    def accumulate(rc, normalize):
        for rs in _row_chunks(tm, rc):
            o = o_ref[rs, :] + mlp(h_ref[rs, :])
            if normalize:
                ms = jnp.mean(o * o, axis=-1, keepdims=True)
                o = o * lax.rsqrt(ms + EPS) * fg_ref[...]
            o_ref[rs, :] = o

    if final_norm:
        @pl.when((f > 0) & (f < last))
        def _():
            accumulate(rc_mid, False)

        @pl.when(f == last)
        def _():
            accumulate(rc_edge, True)
    else:
        @pl.when(f > 0)
        def _():
            accumulate(rc_mid, False)


def _ffn(x2, g, sh, sc, gate, wg, wu, wd, final_g, seq, final_norm, tm=1024, tf=512,
         rc_edge=256, rc_mid=512):
    n, d = x2.shape
    fh = wg.shape[1]
    tpb = seq // tm
    vec = pl.BlockSpec((None, 1, d), lambda i, f: (i // tpb, 0, 0))
    return pl.pallas_call(
        functools.partial(_ffn_kernel, final_norm=final_norm, rc_edge=rc_edge, rc_mid=rc_mid),
        grid=(n // tm, fh // tf),
        in_specs=[
            pl.BlockSpec((tm, d), lambda i, f: (i, 0)),
            pl.BlockSpec((1, d), lambda i, f: (0, 0)),
            vec, vec, vec,
            pl.BlockSpec((d, tf), lambda i, f: (0, f)),
            pl.BlockSpec((d, tf), lambda i, f: (0, f)),
            pl.BlockSpec((tf, d), lambda i, f: (f, 0)),
            pl.BlockSpec((1, d), lambda i, f: (0, 0)),
        ],
        out_specs=pl.BlockSpec((tm, d), lambda i, f: (i, 0)),
        out_shape=jax.ShapeDtypeStruct((n, d), F32),
        scratch_shapes=[pltpu.VMEM((tm, d), BF16)],
        compiler_params=_cparams("parallel", "arbitrary"),
        name="swiglu_ffn",
    )(x2, g, sh, sc, gate, wg, wu, wd, final_g)


def _attn_in_kernel(x_ref, g_ref, sh_ref, sc_ref, w_ref, wf_ref, bf_ref, o_ref, fc_ref,
                    h_ref, carry_ref, *, tiles_per_seq, q_tiles, q_scale, rc):
    i = pl.program_id(0)
    j = pl.program_id(1)

    def emit(h, rs):
        z = _mm(h, w_ref[...]) * jnp.where(j < q_tiles, q_scale, 1.0)
        o_ref[rs, :] = z.astype(o_ref.dtype)

    @pl.when(j == 0)
    def _():
        @pl.when(i % tiles_per_seq == 0)
        def _():
            carry_ref[...] = jnp.zeros_like(carry_ref)

        gm = g_ref[...] * (1.0 + sc_ref[...])
        sh = sh_ref[...]
        row = lax.broadcasted_iota(jnp.int32, (rc, rc), 0)
        col = lax.broadcasted_iota(jnp.int32, (rc, rc), 1)
        tri = (col <= row).astype(BF16)
        for rs in _row_chunks(x_ref.shape[0], rc):
            h = _norm_mod(x_ref[rs, :], gm, sh).astype(BF16)
            h_ref[rs, :] = h
            t = _mm(h, wf_ref[...]) + bf_ref[...]
            log_f = jnp.minimum(t, 0.0) - jnp.log1p(jnp.exp(-jnp.abs(t)))
            hi = log_f.astype(BF16)
            r1 = log_f - hi.astype(F32)
            mid = r1.astype(BF16)
            lo = (r1 - mid.astype(F32)).astype(BF16)
            c2 = _mm(tri, jnp.concatenate([hi, mid], axis=1))
            cum = c2[:, :LANES] + c2[:, LANES:] + _mm(tri, lo) + carry_ref[...]
            fc_ref[rs, :] = cum * LOG2E
            carry_ref[...] = cum[rc - 1:rc, :]
            emit(h, rs)

    @pl.when(j != 0)
    def _():
        emit(h_ref[...], slice(None))


def _attn_in(x2, g, sh, sc, w_qkv, w_f, b_f, seq, tm=1024, tn=1024, rc=256):
    n, d = x2.shape
    nout = w_qkv.shape[1]
    tpb = seq // tm
    kern = functools.partial(_attn_in_kernel, tiles_per_seq=tpb, q_tiles=d // tn,
                             q_scale=LOG2E / math.sqrt(HEAD_DIM), rc=rc)
    return pl.pallas_call(
        kern,
        grid=(n // tm, nout // tn),
        in_specs=[
            pl.BlockSpec((tm, d), lambda i, j: (i, 0)),
            pl.BlockSpec((1, d), lambda i, j: (0, 0)),
            pl.BlockSpec((None, 1, d), lambda i, j: (i // tpb, 0, 0)),
            pl.BlockSpec((None, 1, d), lambda i, j: (i // tpb, 0, 0)),
            pl.BlockSpec((d, tn), lambda i, j: (0, j)),
            pl.BlockSpec((d, LANES), lambda i, j: (0, 0)),
            pl.BlockSpec((1, LANES), lambda i, j: (0, 0)),
        ],
        out_specs=[
            pl.BlockSpec((tm, tn), lambda i, j: (i, j)),
            pl.BlockSpec((tm, LANES), lambda i, j: (i, 0)),
        ],
        out_shape=[
            jax.ShapeDtypeStruct((n, nout), BF16),
            jax.ShapeDtypeStruct((n, LANES), F32),
        ],
        scratch_shapes=[pltpu.VMEM((tm, d), BF16), pltpu.VMEM((1, LANES), F32)],
        compiler_params=_cparams("arbitrary", "arbitrary"),
        name="attn_in_proj",
    )(x2, g, sh, sc, w_qkv, w_f, b_f)


def _fox_kernel(q_ref, k_ref, v_ref, fq_ref, fk_ref, o_ref, va_ref, *, tq, hb, nsplit):
    hg = pl.program_id(1)
    qi = pl.program_id(2)
    rows = tq // nsplit
    dh = HEAD_DIM

    @pl.when(qi == 0)
    def _():
        for hh in range(hb):
            va_ref[hh, :, :dh] = v_ref[:, hh * dh:(hh + 1) * dh]
            va_ref[hh, :, dh:] = jnp.ones((va_ref.shape[1], dh), BF16)

    lane = lax.broadcasted_iota(jnp.int32, fq_ref.shape, 1)
    fq_tile = fq_ref[...]
    fq_heads = [jnp.sum(jnp.where(lane == hg * hb + hh, fq_tile, 0.0), axis=1, keepdims=True)
                for hh in range(hb)]
    chains = [(hh, hf) for hh in range(hb) for hf in range(nsplit)]

    def step(j, carry, masked):
        start = pl.multiple_of(j * tq, tq)
        out = []
        for (hh, hf), (m, acc) in zip(chains, carry):
            ncols = (hf + 1) * rows if masked else tq
            ks = k_ref[pl.ds(start, ncols), hh * dh:(hh + 1) * dh]
            va = va_ref[hh, pl.ds(start, ncols), :]
            fk = fk_ref[hh, pl.ds(j, 1), :][:, :ncols]
            q = q_ref[hf * rows:(hf + 1) * rows, hh * dh:(hh + 1) * dh]
            fq = fq_heads[hh][hf * rows:(hf + 1) * rows, :]
            t = lax.dot_general(q, ks, (((1,), (1,)), ((), ())), preferred_element_type=F32) - fk
            if masked:
                row = lax.broadcasted_iota(jnp.int32, t.shape, 0) + hf * rows
                col = lax.broadcasted_iota(jnp.int32, t.shape, 1)
                t = jnp.where(col <= row, t, -jnp.inf)
            m_new = jnp.maximum(m, jnp.max(t, axis=1, keepdims=True) + fq)
            p = jnp.exp2(t + (fq - m_new))
            alpha = jnp.exp2(m - m_new)
            acc = alpha * acc + _mm(p.astype(BF16), va)
            out.append((m_new, acc))
        return tuple(out)

    init = tuple((jnp.full((rows, 1), -jnp.inf, F32), jnp.zeros((rows, 2 * dh), F32))
                 for _ in chains)
    carry = lax.fori_loop(0, qi, functools.partial(step, masked=False), init)
    carry = step(qi, carry, masked=True)
    for (hh, hf), (_, acc) in zip(chains, carry):
        o_ref[hf * rows:(hf + 1) * rows, hh * dh:(hh + 1) * dh] = (
            acc[:, :dh] / acc[:, dh:]).astype(o_ref.dtype)


def _fox_attention(qkv, fcum, frow, bsz, seq, tq=512, hb=4, nsplit=1):
    n = qkv.shape[0]
    nq = seq // tq
    ng = HEADS // hb
    w = hb * HEAD_DIM
    return pl.pallas_call(
        functools.partial(_fox_kernel, tq=tq, hb=hb, nsplit=nsplit),
        grid=(bsz, ng, nq),
        in_specs=[
            pl.BlockSpec((tq, w), lambda b, g, i: (b * nq + i, g)),
            pl.BlockSpec((seq, w), lambda b, g, i: (b, ng + g)),
            pl.BlockSpec((seq, w), lambda b, g, i: (b, 2 * ng + g)),
            pl.BlockSpec((tq, LANES), lambda b, g, i: (b * nq + i, 0)),
            pl.BlockSpec((hb, nq, tq), lambda b, g, i: (b * ng + g, 0, 0)),
        ],
        out_specs=pl.BlockSpec((tq, w), lambda b, g, i: (b * nq + i, g)),
        out_shape=jax.ShapeDtypeStruct((n, HEADS * HEAD_DIM), BF16),
        scratch_shapes=[pltpu.VMEM((hb, seq, 2 * HEAD_DIM), BF16)],
        compiler_params=_cparams("arbitrary", "arbitrary", "arbitrary"),
        name="fox_attention",
    )(qkv, qkv, qkv, fcum, frow)


def _proj_res_kernel(a_ref, w_ref, x_ref, gate_ref, o_ref):
    o_ref[...] = x_ref[...] + gate_ref[...] * _mm(a_ref[...], w_ref[...])


def _proj_res(a, w, x2, gate, seq, tm=1024, tn=1024):
    n, d = x2.shape
    k = a.shape[1]
    tpb = seq // tm
    return pl.pallas_call(
        _proj_res_kernel,
        grid=(n // tm, d // tn),
        in_specs=[
            pl.BlockSpec((tm, k), lambda i, j: (i, 0)),
            pl.BlockSpec((k, tn), lambda i, j: (0, j)),
            pl.BlockSpec((tm, tn), lambda i, j: (i, j)),
            pl.BlockSpec((None, 1, tn), lambda i, j: (i // tpb, 0, j)),
        ],
        out_specs=pl.BlockSpec((tm, tn), lambda i, j: (i, j)),
        out_shape=jax.ShapeDtypeStruct((n, d), F32),
        compiler_params=_cparams("parallel", "arbitrary"),
        name="attn_out_proj",
    )(a, w, x2, gate)


def kernel(x, c, ada_w, ada_b, norm_mix_g, norm_ffn_g, a_w_in, a_b_in, a_ln_g, a_ln_b, a_w_s,
           a_b_s, a_w_out, b_w_in, b_b_f, b_w_out, ffn_w_gate, ffn_w_up, ffn_w_down, final_g):
    bsz, seq, d = x.shape
    depth = ada_w.shape[0]
    x2 = x.reshape(bsz * seq, d)

    mod = _ada_mod(c, ada_w, ada_b).reshape(depth, bsz, 6, 1, d)
    row = lambda a: a.reshape(1, -1)

    for i in range(depth):
        sh1, sc1, g1, sh2, sc2, g2 = (mod[i, :, k] for k in range(6))
        j = i // 2
        if i % 2 == 0:
            z = _sgu_in(x2, row(norm_mix_g[i]), sh1, sc1, a_w_in[j].astype(BF16),
                        row(a_b_in[j]), seq)
            x2 = _sgu_out(z, row(a_ln_g[j]), row(a_ln_b[j]), a_w_s[j], a_b_s[j].T,
                          a_w_out[j].astype(BF16), x2, g1, seq)
        else:
            w_in = b_w_in[j]
            w_qkv = w_in[:, :3 * d].astype(BF16)
            w_f = jnp.pad(w_in[:, 3 * d:], ((0, 0), (0, LANES - HEADS))).astype(BF16)
            b_f = jnp.pad(b_b_f[j], (0, LANES - HEADS)).reshape(1, LANES)
            qkv, fcum = _attn_in(x2, row(norm_mix_g[i]), sh1, sc1, w_qkv, w_f, b_f, seq)
            tq = 512
            frow = fcum[:, :HEADS].reshape(bsz, seq, HEADS).transpose(0, 2, 1)
            frow = frow.reshape(bsz * HEADS, seq // tq, tq)
            o = _fox_attention(qkv, fcum, frow, bsz, seq, tq=tq)
            x2 = _proj_res(o, b_w_out[j].astype(BF16), x2, g1, seq)
        x2 = _ffn(x2, row(norm_ffn_g[i]), sh2, sc2, g2, ffn_w_gate[i].astype(BF16),
                  ffn_w_up[i].astype(BF16), ffn_w_down[i].astype(BF16), row(final_g), seq,
                  final_norm=(i == depth - 1))
    return x2.reshape(bsz, seq, d)
```

```python
import functools
import math

import jax
import jax.numpy as jnp
from jax import lax
from jax.experimental import pallas as pl
from jax.experimental.pallas import tpu as pltpu

F32 = jnp.float32
BF16 = jnp.bfloat16

EPS = 1e-6
CHUNK = 128
GROUPS = 16
HEADS = 16
HEAD_DIM = 128
LANES = 128
LOG2E = math.log2(math.e)
ONES_ROWS = 16
VMEM_LIMIT = 56 * 1024 * 1024


def _cparams(*sem):
    return pltpu.CompilerParams(dimension_semantics=sem, vmem_limit_bytes=VMEM_LIMIT)


def _row_chunks(tm, rc):
    return [slice(r * rc, (r + 1) * rc) for r in range(tm // rc)]


def _norm_mod(x, gm, sh):
    ms = jnp.mean(x * x, axis=-1, keepdims=True)
    return x * lax.rsqrt(ms + EPS) * gm + sh


def _mm(a, b):
    return jnp.dot(a, b, preferred_element_type=F32)


def _ada_kernel(ct_ref, w_ref, b_ref, o_ref):
    ct = ct_ref[...]
    ca = ct * jax.nn.sigmoid(ct)
    w = w_ref[...]
    rows = []
    for b in range(ct.shape[1]):
        rows.append(jnp.sum(w * ca[:, b:b + 1], axis=0, keepdims=True))
    o_ref[...] = jnp.concatenate(rows, axis=0) + b_ref[...]


def _ada_mod(c, ada_w, ada_b, tn=512):
    depth, d, n = ada_w.shape
    bsz = c.shape[0]
    return pl.pallas_call(
        _ada_kernel,
        grid=(depth, n // tn),
        in_specs=[
            pl.BlockSpec((d, bsz), lambda l, j: (0, 0)),
            pl.BlockSpec((None, d, tn), lambda l, j: (l, 0, j)),
            pl.BlockSpec((None, 1, tn), lambda l, j: (l, 0, j)),
        ],
        out_specs=pl.BlockSpec((None, bsz, tn), lambda l, j: (l, 0, j)),
        out_shape=jax.ShapeDtypeStruct((depth, bsz, n), F32),
        compiler_params=_cparams("parallel", "parallel"),
        name="adaln_mod",
    )(c.T, ada_w, ada_b.reshape(depth, 1, n))


def _sgu_in_kernel(x_ref, g_ref, sh_ref, sc_ref, w_ref, b_ref, o_ref, h_ref, *, rc):
    j = pl.program_id(1)

    def emit(h, rs):
        z = _mm(h, w_ref[...]) + b_ref[...]
        z = 0.5 * z * (1.0 + lax.erf(z * (1.0 / math.sqrt(2.0))))
        o_ref[rs, :] = z.astype(o_ref.dtype)

    @pl.when(j == 0)
    def _():
        gm = g_ref[...] * (1.0 + sc_ref[...])
        sh = sh_ref[...]
        for rs in _row_chunks(x_ref.shape[0], rc):
            h = _norm_mod(x_ref[rs, :], gm, sh).astype(BF16)
            h_ref[rs, :] = h
            emit(h, rs)

    @pl.when(j != 0)
    def _():
        emit(h_ref[...], slice(None))


def _sgu_in(x2, g, sh, sc, w, b, seq, tm=1024, tn=1024, rc=256):
    n, d = x2.shape
    nout = w.shape[1]
    tpb = seq // tm
    return pl.pallas_call(
        functools.partial(_sgu_in_kernel, rc=rc),
        grid=(n // tm, nout // tn),
        in_specs=[
            pl.BlockSpec((tm, d), lambda i, j: (i, 0)),
            pl.BlockSpec((1, d), lambda i, j: (0, 0)),
            pl.BlockSpec((None, 1, d), lambda i, j: (i // tpb, 0, 0)),
            pl.BlockSpec((None, 1, d), lambda i, j: (i // tpb, 0, 0)),
            pl.BlockSpec((d, tn), lambda i, j: (0, j)),
            pl.BlockSpec((1, tn), lambda i, j: (0, j)),
        ],
        out_specs=pl.BlockSpec((tm, tn), lambda i, j: (i, j)),
        out_shape=jax.ShapeDtypeStruct((n, nout), BF16),
        scratch_shapes=[pltpu.VMEM((tm, d), BF16)],
        compiler_params=_cparams("parallel", "arbitrary"),
        name="sgu_in_proj",
    )(x2, g, sh, sc, w, b)


def _sgu_out_kernel(u_ref, v_ref, lg_ref, lb_ref, ws_ref, bst_ref, w_ref, x_ref, gate_ref,
                    o_ref, y_ref, wm_ref, *, rc):
    tm, aw = u_ref.shape
    j = pl.program_id(1)
    nck = rc // CHUNK

    def emit(y, rs):
        o_ref[rs, :] = x_ref[rs, :] + gate_ref[...] * _mm(y, w_ref[...])

    @pl.when(j == 0)
    def _():
        row = lax.broadcasted_iota(jnp.int32, (CHUNK, CHUNK), 0)
        col = lax.broadcasted_iota(jnp.int32, (CHUNK, CHUNK), 1)
        causal = col <= row
        for g in range(GROUPS):
            wm_ref[g] = jnp.where(causal, ws_ref[g], 0.0).astype(BF16)
        lg = lg_ref[...]
        lb = lb_ref[...]
        for rs in _row_chunks(tm, rc):
            v = v_ref[rs, :].astype(F32)
            mu = jnp.mean(v, axis=-1, keepdims=True)
            vc = v - mu
            var = jnp.mean(vc * vc, axis=-1, keepdims=True)
            vn = (vc * lax.rsqrt(var + EPS) * lg + lb).astype(BF16)
            u = u_ref[rs, :]
            for g in range(GROUPS):
                cs = slice(g * CHUNK, (g + 1) * CHUNK)
                vg = jnp.concatenate(
                    [vn[c * CHUNK:(c + 1) * CHUNK, cs] for c in range(nck)], axis=1)
                sv = _mm(wm_ref[g], vg) + bst_ref[:, g:g + 1]
                for c in range(nck):
                    r0 = rs.start + c * CHUNK
                    ug = u[c * CHUNK:(c + 1) * CHUNK, cs].astype(F32)
                    y_ref[r0:r0 + CHUNK, cs] = (
                        ug * sv[:, c * CHUNK:(c + 1) * CHUNK]).astype(BF16)
            emit(y_ref[rs, :], rs)

    @pl.when(j != 0)
    def _():
        emit(y_ref[...], slice(None))


def _sgu_out(z, ln_g, ln_b, w_s, b_s_t, w_out, x2, gate, seq, tm=1024, tn=512, rc=256):
    n, d = x2.shape
    aw = w_out.shape[0]
    tpb = seq // tm
    return pl.pallas_call(
        functools.partial(_sgu_out_kernel, rc=rc),
        grid=(n // tm, d // tn),
        in_specs=[
            pl.BlockSpec((tm, aw), lambda i, j: (i, 0)),
            pl.BlockSpec((tm, aw), lambda i, j: (i, 1)),
            pl.BlockSpec((1, aw), lambda i, j: (0, 0)),
            pl.BlockSpec((1, aw), lambda i, j: (0, 0)),
            pl.BlockSpec((GROUPS, CHUNK, CHUNK), lambda i, j: (0, 0, 0)),
            pl.BlockSpec((CHUNK, GROUPS), lambda i, j: (0, 0)),
            pl.BlockSpec((aw, tn), lambda i, j: (0, j)),
            pl.BlockSpec((tm, tn), lambda i, j: (i, j)),
            pl.BlockSpec((None, 1, tn), lambda i, j: (i // tpb, 0, j)),
        ],
        out_specs=pl.BlockSpec((tm, tn), lambda i, j: (i, j)),
        out_shape=jax.ShapeDtypeStruct((n, d), F32),
        scratch_shapes=[pltpu.VMEM((tm, aw), BF16), pltpu.VMEM((GROUPS, CHUNK, CHUNK), BF16)],
        compiler_params=_cparams("parallel", "arbitrary"),
        name="sgu_mix_out_proj",
    )(z, z, ln_g, ln_b, w_s, b_s_t, w_out, x2, gate)


def _ffn_kernel(x_ref, g_ref, sh_ref, sc_ref, gate_ref, wg_ref, wu_ref, wd_ref, fg_ref,
                o_ref, h_ref, *, final_norm, rc_edge, rc_mid):
    f = pl.program_id(1)
    last = pl.num_programs(1) - 1
    tm = x_ref.shape[0]

    def mlp(h):
        a = _mm(h, wg_ref[...])
        b = _mm(h, wu_ref[...])
        hid = (a * jax.nn.sigmoid(a) * b).astype(BF16)
        return gate_ref[...] * _mm(hid, wd_ref[...])

    @pl.when(f == 0)
    def _():
        gm = g_ref[...] * (1.0 + sc_ref[...])
        sh = sh_ref[...]
        for rs in _row_chunks(tm, rc_edge):
            x = x_ref[rs, :]
            h = _norm_mod(x, gm, sh).astype(BF16)
            h_ref[rs, :] = h
            o_ref[rs, :] = x + mlp(h)

    def accumulate(rc, normalize):
        for rs in _row_chunks(tm, rc):
            o = o_ref[rs, :] + mlp(h_ref[rs, :])
            if normalize:
                ms = jnp.mean(o * o, axis=-1, keepdims=True)
                o = o * lax.rsqrt(ms + EPS) * fg_ref[...]
            o_ref[rs, :] = o

    if final_norm:
        @pl.when((f > 0) & (f < last))
        def _():
            accumulate(rc_mid, False)

        @pl.when(f == last)
        def _():
            accumulate(rc_edge, True)
    else:
        @pl.when(f > 0)
        def _():
            accumulate(rc_mid, False)


def _ffn(x2, g, sh, sc, gate, wg, wu, wd, final_g, seq, final_norm, tm=1024, tf=512,
         rc_edge=256, rc_mid=512):
    n, d = x2.shape
    fh = wg.shape[1]
    tpb = seq // tm
    vec = pl.BlockSpec((None, 1, d), lambda i, f: (i // tpb, 0, 0))
    return pl.pallas_call(
        functools.partial(_ffn_kernel, final_norm=final_norm, rc_edge=rc_edge, rc_mid=rc_mid),
        grid=(n // tm, fh // tf),
        in_specs=[
            pl.BlockSpec((tm, d), lambda i, f: (i, 0)),
            pl.BlockSpec((1, d), lambda i, f: (0, 0)),
            vec, vec, vec,
            pl.BlockSpec((d, tf), lambda i, f: (0, f)),
            pl.BlockSpec((d, tf), lambda i, f: (0, f)),
            pl.BlockSpec((tf, d), lambda i, f: (f, 0)),
            pl.BlockSpec((1, d), lambda i, f: (0, 0)),
        ],
        out_specs=pl.BlockSpec((tm, d), lambda i, f: (i, 0)),
        out_shape=jax.ShapeDtypeStruct((n, d), F32),
        scratch_shapes=[pltpu.VMEM((tm, d), BF16)],
        compiler_params=_cparams("parallel", "arbitrary"),
        name="swiglu_ffn",
    )(x2, g, sh, sc, gate, wg, wu, wd, final_g)


def _attn_in_kernel(x_ref, g_ref, sh_ref, sc_ref, w_ref, wf_ref, bf_ref, o_ref, fc_ref,
                    h_ref, carry_ref, *, tiles_per_seq, q_tiles, q_scale, rc):
    i = pl.program_id(0)
    j = pl.program_id(1)

    def emit(h, rs):
        z = _mm(h, w_ref[...]) * jnp.where(j < q_tiles, q_scale, 1.0)
        o_ref[rs, :] = z.astype(o_ref.dtype)

    @pl.when(j == 0)
    def _():
        @pl.when(i % tiles_per_seq == 0)
        def _():
            carry_ref[...] = jnp.zeros_like(carry_ref)

        gm = g_ref[...] * (1.0 + sc_ref[...])
        sh = sh_ref[...]
        row = lax.broadcasted_iota(jnp.int32, (rc, rc), 0)
        col = lax.broadcasted_iota(jnp.int32, (rc, rc), 1)
        tri = (col <= row).astype(BF16)
        for rs in _row_chunks(x_ref.shape[0], rc):
            h = _norm_mod(x_ref[rs, :], gm, sh).astype(BF16)
            h_ref[rs, :] = h
            t = _mm(h, wf_ref[...]) + bf_ref[...]
            log_f = jnp.minimum(t, 0.0) - jnp.log1p(jnp.exp(-jnp.abs(t)))
            hi = log_f.astype(BF16)
            r1 = log_f - hi.astype(F32)
            mid = r1.astype(BF16)
            lo = (r1 - mid.astype(F32)).astype(BF16)
            c2 = _mm(tri, jnp.concatenate([hi, mid], axis=1))
            cum = c2[:, :LANES] + c2[:, LANES:] + _mm(tri, lo) + carry_ref[...]
            fc_ref[rs, :] = cum * LOG2E
            carry_ref[...] = cum[rc - 1:rc, :]
            emit(h, rs)

    @pl.when(j != 0)
    def _():
        emit(h_ref[...], slice(None))


def _attn_in(x2, g, sh, sc, w_qkv, w_f, b_f, seq, tm=1024, tn=1024, rc=256):
    n, d = x2.shape
    nout = w_qkv.shape[1]
    tpb = seq // tm
    kern = functools.partial(_attn_in_kernel, tiles_per_seq=tpb, q_tiles=d // tn,
                             q_scale=LOG2E / math.sqrt(HEAD_DIM), rc=rc)
    return pl.pallas_call(
        kern,
        grid=(n // tm, nout // tn),
        in_specs=[
            pl.BlockSpec((tm, d), lambda i, j: (i, 0)),
            pl.BlockSpec((1, d), lambda i, j: (0, 0)),
            pl.BlockSpec((None, 1, d), lambda i, j: (i // tpb, 0, 0)),
            pl.BlockSpec((None, 1, d), lambda i, j: (i // tpb, 0, 0)),
            pl.BlockSpec((d, tn), lambda i, j: (0, j)),
            pl.BlockSpec((d, LANES), lambda i, j: (0, 0)),
            pl.BlockSpec((1, LANES), lambda i, j: (0, 0)),
        ],
        out_specs=[
            pl.BlockSpec((tm, tn), lambda i, j: (i, j)),
            pl.BlockSpec((tm, LANES), lambda i, j: (i, 0)),
        ],
        out_shape=[
            jax.ShapeDtypeStruct((n, nout), BF16),
            jax.ShapeDtypeStruct((n, LANES), F32),
        ],
        scratch_shapes=[pltpu.VMEM((tm, d), BF16), pltpu.VMEM((1, LANES), F32)],
        compiler_params=_cparams("arbitrary", "arbitrary"),
        name="attn_in_proj",
    )(x2, g, sh, sc, w_qkv, w_f, b_f)


def _fox_kernel(q_ref, k_ref, vt_ref, fc_ref, fr_ref, o_ref, fkb_ref, va_ref, *, tq, hb):
    hg = pl.program_id(1)
    qi = pl.program_id(2)
    dh = HEAD_DIM
    nl = tq // LANES
    seq = k_ref.shape[0]
    da = dh + ONES_ROWS

    @pl.when(qi == 0)
    def _():
        lane = lax.broadcasted_iota(jnp.int32, (tq, LANES), 1)
        for c in range(seq // tq):
            rs = slice(c * tq, (c + 1) * tq)
            fc = fc_ref[rs, :]
            for hh in range(hb):
                col = jnp.sum(jnp.where(lane == hg * hb + hh, fc, 0.0), axis=1, keepdims=True)
                fkb_ref[hh, rs, :] = jnp.broadcast_to(col, (tq, LANES))
                va_ref[c, hh * da:hh * da + dh, :] = vt_ref[c, hh * dh:(hh + 1) * dh, :]
                va_ref[c, hh * da + dh:(hh + 1) * da, :] = jnp.ones((ONES_ROWS, tq), BF16)

    def step(c0, nck, carry, masked):
        tk = nck * tq
        start = pl.multiple_of(c0 * tq, tq)
        ts = []
        for hh in range(hb):
            ks = k_ref[pl.ds(start, tk), hh * dh:(hh + 1) * dh]
            q = q_ref[:, hh * dh:(hh + 1) * dh]
            fkb = fkb_ref[hh, pl.ds(start, tk), :]
            t = lax.dot_general(ks, q, (((1,), (1,)), ((), ())), preferred_element_type=F32)
            t = t - jnp.concatenate([fkb] * nl, axis=1)
            if masked:
                kpos = lax.broadcasted_iota(jnp.int32, t.shape, 0)
                qpos = lax.broadcasted_iota(jnp.int32, t.shape, 1) + (nck - 1) * tq
                t = jnp.where(kpos <= qpos, t, -jnp.inf)
            ts.append(t)
        ps = []
        for hh, (m, acc) in enumerate(carry):
            fq = fr_ref[hh, pl.ds(qi, 1), :]
            m_new = jnp.maximum(m, jnp.max(ts[hh], axis=0, keepdims=True) + fq)
            p = jnp.exp2(ts[hh] + (fq - m_new))
            alpha = jnp.exp2(m - m_new)
            ps.append((m_new, alpha, p.astype(BF16)))
        out = []
        for hh, (m, acc) in enumerate(carry):
            m_new, alpha, p = ps[hh]
            va = jnp.concatenate([va_ref[c0 + c, hh * da:(hh + 1) * da, :] for c in range(nck)],
                                 axis=1)
            out.append((m_new, alpha * acc + _mm(va, p)))
        return tuple(out)

    def finish(carry):
        for hh, (_, acc) in enumerate(carry):
            o_ref[:, hh * dh:(hh + 1) * dh] = (acc[:dh] / acc[dh:dh + 1]).T.astype(o_ref.dtype)

    init = tuple((jnp.full((1, tq), -jnp.inf, F32), jnp.zeros((da, tq), F32))
                 for _ in range(hb))
    carry = lax.fori_loop(0, qi // 2, lambda jj, c: step(2 * jj, 2, c, False), init)

    @pl.when(qi % 2 == 1)
    def _():
        finish(step(qi - 1, 2, carry, True))

    @pl.when(qi % 2 == 0)
    def _():
        finish(step(qi, 1, carry, True))


def _fox_attention(qkv, vt, fcum, frow, bsz, seq, tq=512, hb=4):
    n = qkv.shape[0]
    nq = seq // tq
    ng = HEADS // hb
    w = hb * HEAD_DIM
    return pl.pallas_call(
        functools.partial(_fox_kernel, tq=tq, hb=hb),
        grid=(bsz, ng, nq),
        in_specs=[
            pl.BlockSpec((tq, w), lambda b, g, i: (b * nq + i, g)),
            pl.BlockSpec((seq, w), lambda b, g, i: (b, ng + g)),
            pl.BlockSpec((nq, w, tq), lambda b, g, i: (b, g, 0)),
            pl.BlockSpec((seq, LANES), lambda b, g, i: (b, 0)),
            pl.BlockSpec((hb, nq, tq), lambda b, g, i: (b * ng + g, 0, 0)),
        ],
        out_specs=pl.BlockSpec((tq, w), lambda b, g, i: (b * nq + i, g)),
        out_shape=jax.ShapeDtypeStruct((n, HEADS * HEAD_DIM), BF16),
        scratch_shapes=[pltpu.VMEM((hb, seq, LANES), F32),
                        pltpu.VMEM((nq, hb * (HEAD_DIM + ONES_ROWS), tq), BF16)],
        compiler_params=_cparams("arbitrary", "arbitrary", "arbitrary"),
        name="fox_attention",
    )(qkv, qkv, vt, fcum, frow)


def _proj_res_kernel(a_ref, w_ref, x_ref, gate_ref, o_ref):
    o_ref[...] = x_ref[...] + gate_ref[...] * _mm(a_ref[...], w_ref[...])


def _proj_res(a, w, x2, gate, seq, tm=1024, tn=1024):
    n, d = x2.shape
    k = a.shape[1]
    tpb = seq // tm
    return pl.pallas_call(
        _proj_res_kernel,
        grid=(n // tm, d // tn),
        in_specs=[
            pl.BlockSpec((tm, k), lambda i, j: (i, 0)),
            pl.BlockSpec((k, tn), lambda i, j: (0, j)),
            pl.BlockSpec((tm, tn), lambda i, j: (i, j)),
            pl.BlockSpec((None, 1, tn), lambda i, j: (i // tpb, 0, j)),
        ],
        out_specs=pl.BlockSpec((tm, tn), lambda i, j: (i, j)),
        out_shape=jax.ShapeDtypeStruct((n, d), F32),
        compiler_params=_cparams("parallel", "arbitrary"),
        name="attn_out_proj",
    )(a, w, x2, gate)


def kernel(x, c, ada_w, ada_b, norm_mix_g, norm_ffn_g, a_w_in, a_b_in, a_ln_g, a_ln_b, a_w_s,
           a_b_s, a_w_out, b_w_in, b_b_f, b_w_out, ffn_w_gate, ffn_w_up, ffn_w_down, final_g):
    bsz, seq, d = x.shape
    depth = ada_w.shape[0]
    x2 = x.reshape(bsz * seq, d)

    mod = _ada_mod(c, ada_w, ada_b).reshape(depth, bsz, 6, 1, d)
    row = lambda a: a.reshape(1, -1)

    for i in range(depth):
        sh1, sc1, g1, sh2, sc2, g2 = (mod[i, :, k] for k in range(6))
        j = i // 2
        if i % 2 == 0:
            z = _sgu_in(x2, row(norm_mix_g[i]), sh1, sc1, a_w_in[j].astype(BF16),
                        row(a_b_in[j]), seq)
            x2 = _sgu_out(z, row(a_ln_g[j]), row(a_ln_b[j]), a_w_s[j], a_b_s[j].T,
                          a_w_out[j].astype(BF16), x2, g1, seq)
        else:
            w_in = b_w_in[j]
            w_qkv = w_in[:, :3 * d].astype(BF16)
            w_f = jnp.pad(w_in[:, 3 * d:], ((0, 0), (0, LANES - HEADS))).astype(BF16)
            b_f = jnp.pad(b_b_f[j], (0, LANES - HEADS)).reshape(1, LANES)
            qkv, fcum = _attn_in(x2, row(norm_mix_g[i]), sh1, sc1, w_qkv, w_f, b_f, seq)
            tq = 512
            frow = fcum[:, :HEADS].reshape(bsz, seq, HEADS).transpose(0, 2, 1)
            frow = frow.reshape(bsz * HEADS, seq // tq, tq)
            vt = qkv[:, 2 * d:].reshape(bsz * seq // tq, tq, d).transpose(0, 2, 1)
            o = _fox_attention(qkv, vt, fcum, frow, bsz, seq, tq=tq)
            x2 = _proj_res(o, b_w_out[j].astype(BF16), x2, g1, seq)
        x2 = _ffn(x2, row(norm_ffn_g[i]), sh2, sc2, g2, ffn_w_gate[i].astype(BF16),
                  ffn_w_up[i].astype(BF16), ffn_w_down[i].astype(BF16), row(final_g), seq,
                  final_norm=(i == depth - 1))
    return x2.reshape(bsz, seq, d)
```

```python
import functools
import math

import jax
import jax.numpy as jnp
from jax import lax
from jax.experimental import pallas as pl
from jax.experimental.pallas import tpu as pltpu

F32 = jnp.float32
BF16 = jnp.bfloat16

EPS = 1e-6
CHUNK = 128
GROUPS = 16
HEADS = 16
HEAD_DIM = 128
LANES = 128
LOG2E = math.log2(math.e)
ONES_ROWS = 16
VMEM_LIMIT = 56 * 1024 * 1024


def _cparams(*sem):
    return pltpu.CompilerParams(dimension_semantics=sem, vmem_limit_bytes=VMEM_LIMIT)


def _row_chunks(tm, rc):
    return [slice(r * rc, (r + 1) * rc) for r in range(tm // rc)]


def _norm_mod(x, gm, sh):
    ms = jnp.mean(x * x, axis=-1, keepdims=True)
    return x * lax.rsqrt(ms + EPS) * gm + sh


def _mm(a, b):
    return jnp.dot(a, b, preferred_element_type=F32)


def _ada_kernel(ct_ref, w_ref, b_ref, o_ref):
    ct = ct_ref[...]
    ca = ct * jax.nn.sigmoid(ct)
    w = w_ref[...]
    rows = []
    for b in range(ct.shape[1]):
        rows.append(jnp.sum(w * ca[:, b:b + 1], axis=0, keepdims=True))
    o_ref[...] = jnp.concatenate(rows, axis=0) + b_ref[...]


def _ada_mod(c, ada_w, ada_b, tn=512):
    depth, d, n = ada_w.shape
    bsz = c.shape[0]
    return pl.pallas_call(
        _ada_kernel,
        grid=(depth, n // tn),
        in_specs=[
            pl.BlockSpec((d, bsz), lambda l, j: (0, 0)),
            pl.BlockSpec((None, d, tn), lambda l, j: (l, 0, j)),
            pl.BlockSpec((None, 1, tn), lambda l, j: (l, 0, j)),
        ],
        out_specs=pl.BlockSpec((None, bsz, tn), lambda l, j: (l, 0, j)),
        out_shape=jax.ShapeDtypeStruct((depth, bsz, n), F32),
        compiler_params=_cparams("parallel", "parallel"),
        name="adaln_mod",
    )(c.T, ada_w, ada_b.reshape(depth, 1, n))


def _sgu_in_kernel(x_ref, g_ref, sh_ref, sc_ref, w_ref, b_ref, o_ref, h_ref, *, rc):
    j = pl.program_id(1)

    def emit(h, rs):
        z = _mm(h, w_ref[...]) + b_ref[...]
        z = 0.5 * z * (1.0 + lax.erf(z * (1.0 / math.sqrt(2.0))))
        o_ref[rs, :] = z.astype(o_ref.dtype)

    @pl.when(j == 0)
    def _():
        gm = g_ref[...] * (1.0 + sc_ref[...])
        sh = sh_ref[...]
        for rs in _row_chunks(x_ref.shape[0], rc):
            h = _norm_mod(x_ref[rs, :], gm, sh).astype(BF16)
            h_ref[rs, :] = h
            emit(h, rs)

    @pl.when(j != 0)
    def _():
        emit(h_ref[...], slice(None))


def _sgu_in(x2, g, sh, sc, w, b, seq, tm=1024, tn=1024, rc=256):
    n, d = x2.shape
    nout = w.shape[1]
    tpb = seq // tm
    return pl.pallas_call(
        functools.partial(_sgu_in_kernel, rc=rc),
        grid=(n // tm, nout // tn),
        in_specs=[
            pl.BlockSpec((tm, d), lambda i, j: (i, 0)),
            pl.BlockSpec((1, d), lambda i, j: (0, 0)),
            pl.BlockSpec((None, 1, d), lambda i, j: (i // tpb, 0, 0)),
            pl.BlockSpec((None, 1, d), lambda i, j: (i // tpb, 0, 0)),
            pl.BlockSpec((d, tn), lambda i, j: (0, j)),
            pl.BlockSpec((1, tn), lambda i, j: (0, j)),
        ],
        out_specs=pl.BlockSpec((tm, tn), lambda i, j: (i, j)),
        out_shape=jax.ShapeDtypeStruct((n, nout), BF16),
        scratch_shapes=[pltpu.VMEM((tm, d), BF16)],
        compiler_params=_cparams("parallel", "arbitrary"),
        name="sgu_in_proj",
    )(x2, g, sh, sc, w, b)


def _sgu_out_kernel(u_ref, v_ref, lg_ref, lb_ref, ws_ref, bst_ref, w_ref, x_ref, gate_ref,
                    o_ref, y_ref, wm_ref, *, rc):
    tm, aw = u_ref.shape
    j = pl.program_id(1)
    nck = rc // CHUNK

    def emit(y, rs):
        o_ref[rs, :] = x_ref[rs, :] + gate_ref[...] * _mm(y, w_ref[...])

    @pl.when(j == 0)
    def _():
        row = lax.broadcasted_iota(jnp.int32, (CHUNK, CHUNK), 0)
        col = lax.broadcasted_iota(jnp.int32, (CHUNK, CHUNK), 1)
        causal = col <= row
        for g in range(GROUPS):
            wm_ref[g] = jnp.where(causal, ws_ref[g], 0.0).astype(BF16)
        lg = lg_ref[...]
        lb = lb_ref[...]
        for rs in _row_chunks(tm, rc):
            v = v_ref[rs, :].astype(F32)
            mu = jnp.mean(v, axis=-1, keepdims=True)
            vc = v - mu
            var = jnp.mean(vc * vc, axis=-1, keepdims=True)
            vn = (vc * lax.rsqrt(var + EPS) * lg + lb).astype(BF16)
            u = u_ref[rs, :]
            for g in range(GROUPS):
                cs = slice(g * CHUNK, (g + 1) * CHUNK)
                vg = jnp.concatenate(
                    [vn[c * CHUNK:(c + 1) * CHUNK, cs] for c in range(nck)], axis=1)
                sv = _mm(wm_ref[g], vg) + bst_ref[:, g:g + 1]
                for c in range(nck):
                    r0 = rs.start + c * CHUNK
                    ug = u[c * CHUNK:(c + 1) * CHUNK, cs].astype(F32)
                    y_ref[r0:r0 + CHUNK, cs] = (
                        ug * sv[:, c * CHUNK:(c + 1) * CHUNK]).astype(BF16)
            emit(y_ref[rs, :], rs)

    @pl.when(j != 0)
    def _():
        emit(y_ref[...], slice(None))


def _sgu_out(z, ln_g, ln_b, w_s, b_s_t, w_out, x2, gate, seq, tm=1024, tn=512, rc=256):
    n, d = x2.shape
    aw = w_out.shape[0]
    tpb = seq // tm
    return pl.pallas_call(
        functools.partial(_sgu_out_kernel, rc=rc),
        grid=(n // tm, d // tn),
        in_specs=[
            pl.BlockSpec((tm, aw), lambda i, j: (i, 0)),
            pl.BlockSpec((tm, aw), lambda i, j: (i, 1)),
            pl.BlockSpec((1, aw), lambda i, j: (0, 0)),
            pl.BlockSpec((1, aw), lambda i, j: (0, 0)),
            pl.BlockSpec((GROUPS, CHUNK, CHUNK), lambda i, j: (0, 0, 0)),
            pl.BlockSpec((CHUNK, GROUPS), lambda i, j: (0, 0)),
            pl.BlockSpec((aw, tn), lambda i, j: (0, j)),
            pl.BlockSpec((tm, tn), lambda i, j: (i, j)),
            pl.BlockSpec((None, 1, tn), lambda i, j: (i // tpb, 0, j)),
        ],
        out_specs=pl.BlockSpec((tm, tn), lambda i, j: (i, j)),
        out_shape=jax.ShapeDtypeStruct((n, d), F32),
        scratch_shapes=[pltpu.VMEM((tm, aw), BF16), pltpu.VMEM((GROUPS, CHUNK, CHUNK), BF16)],
        compiler_params=_cparams("parallel", "arbitrary"),
        name="sgu_mix_out_proj",
    )(z, z, ln_g, ln_b, w_s, b_s_t, w_out, x2, gate)


def _ffn_kernel(x_ref, g_ref, sh_ref, sc_ref, gate_ref, wg_ref, wu_ref, wd_ref, fg_ref,
                o_ref, h_ref, *, final_norm, rc_edge, rc_mid):
    f = pl.program_id(1)
    last = pl.num_programs(1) - 1
    tm = x_ref.shape[0]

    def mlp(h):
        a = _mm(h, wg_ref[...])
        b = _mm(h, wu_ref[...])
        hid = (a * jax.nn.sigmoid(a) * b).astype(BF16)
        return gate_ref[...] * _mm(hid, wd_ref[...])

    @pl.when(f == 0)
    def _():
        gm = g_ref[...] * (1.0 + sc_ref[...])
        sh = sh_ref[...]
        for rs in _row_chunks(tm, rc_edge):
            x = x_ref[rs, :]
            h = _norm_mod(x, gm, sh).astype(BF16)
            h_ref[rs, :] = h
            o_ref[rs, :] = x + mlp(h)

    def accumulate(rc, normalize):
        for rs in _row_chunks(tm, rc):
            o = o_ref[rs, :] + mlp(h_ref[rs, :])
            if normalize:
                ms = jnp.mean(o * o, axis=-1, keepdims=True)
                o = o * lax.rsqrt(ms + EPS) * fg_ref[...]
            o_ref[rs, :] = o

    if final_norm:
        @pl.when((f > 0) & (f < last))
        def _():
            accumulate(rc_mid, False)

        @pl.when(f == last)
        def _():
            accumulate(rc_edge, True)
    else:
        @pl.when(f > 0)
        def _():
            accumulate(rc_mid, False)


def _ffn(x2, g, sh, sc, gate, wg, wu, wd, layer, final_g, seq, final_norm, tm=1024, tf=512,
         rc_edge=256, rc_mid=512):
    n, d = x2.shape
    fh = wg.shape[2]
    tpb = seq // tm
    vec = pl.BlockSpec((None, 1, d), lambda i, f: (i // tpb, 0, 0))
    return pl.pallas_call(
        functools.partial(_ffn_kernel, final_norm=final_norm, rc_edge=rc_edge, rc_mid=rc_mid),
        grid=(n // tm, fh // tf),
        in_specs=[
            pl.BlockSpec((tm, d), lambda i, f: (i, 0)),
            pl.BlockSpec((1, d), lambda i, f: (0, 0)),
            vec, vec, vec,
            pl.BlockSpec((None, d, tf), lambda i, f: (layer, 0, f)),
            pl.BlockSpec((None, d, tf), lambda i, f: (layer, 0, f)),
            pl.BlockSpec((None, tf, d), lambda i, f: (layer, f, 0)),
            pl.BlockSpec((1, d), lambda i, f: (0, 0)),
        ],
        out_specs=pl.BlockSpec((tm, d), lambda i, f: (i, 0)),
        out_shape=jax.ShapeDtypeStruct((n, d), F32),
        scratch_shapes=[pltpu.VMEM((tm, d), BF16)],
        compiler_params=_cparams("parallel", "arbitrary"),
        name="swiglu_ffn",
    )(x2, g, sh, sc, gate, wg, wu, wd, final_g)


def _attn_in_kernel(x_ref, g_ref, sh_ref, sc_ref, w_ref, wvt_ref, wf_ref, bf_ref,
                    o_ref, vt_ref, fc_ref, h_ref, carry_ref, *,
                    tiles_per_seq, q_tiles, qk_tiles, q_scale, rc, tq):
    i = pl.program_id(0)
    j = pl.program_id(1)
    tm = x_ref.shape[0]

    def emit(h, rs):
        z = _mm(h, w_ref[...]) * jnp.where(j < q_tiles, q_scale, 1.0)
        o_ref[rs, :] = z.astype(o_ref.dtype)

    @pl.when(j == 0)
    def _():
        @pl.when(i % tiles_per_seq == 0)
        def _():
            carry_ref[...] = jnp.zeros_like(carry_ref)

        gm = g_ref[...] * (1.0 + sc_ref[...])
        sh = sh_ref[...]
        row = lax.broadcasted_iota(jnp.int32, (rc, rc), 0)
        col = lax.broadcasted_iota(jnp.int32, (rc, rc), 1)
        tri = (col <= row).astype(BF16)
        for rs in _row_chunks(tm, rc):
            h = _norm_mod(x_ref[rs, :], gm, sh).astype(BF16)
            h_ref[rs, :] = h
            t = _mm(h, wf_ref[...]) + bf_ref[...]
            log_f = jnp.minimum(t, 0.0) - jnp.log1p(jnp.exp(-jnp.abs(t)))
            hi = log_f.astype(BF16)
            r1 = log_f - hi.astype(F32)
            mid = r1.astype(BF16)
            lo = (r1 - mid.astype(F32)).astype(BF16)
            c2 = _mm(tri, jnp.concatenate([hi, mid], axis=1))
            cum = c2[:, :LANES] + c2[:, LANES:] + _mm(tri, lo) + carry_ref[...]
            fc_ref[rs, :] = cum * LOG2E
            carry_ref[...] = cum[rc - 1:rc, :]
            emit(h, rs)

    @pl.when((j != 0) & (j < qk_tiles))
    def _():
        emit(h_ref[...], slice(None))

    @pl.when(j >= qk_tiles)
    def _():
        zt = lax.dot_general(wvt_ref[...], h_ref[...], (((1,), (1,)), ((), ())),
                             preferred_element_type=F32)
        for c in range(tm // tq):
            vt_ref[c] = zt[:, c * tq:(c + 1) * tq].astype(vt_ref.dtype)


def _attn_in(x2, g, sh, sc, w_all, layer, w_vt, w_f, b_f, seq, tq, tm=1024, tn=1024, rc=256):
    n, d = x2.shape
    tpb = seq // tm
    qk_tiles = 2 * d // tn
    v_tiles = d // tn
    kern = functools.partial(_attn_in_kernel, tiles_per_seq=tpb, q_tiles=d // tn,
                             qk_tiles=qk_tiles, q_scale=LOG2E / math.sqrt(HEAD_DIM), rc=rc, tq=tq)
    qk_col = lambda j: jnp.minimum(j, qk_tiles - 1)
    v_row = lambda j: jnp.maximum(j - qk_tiles, 0)
    return pl.pallas_call(
        kern,
        grid=(n // tm, qk_tiles + v_tiles),
        in_specs=[
            pl.BlockSpec((tm, d), lambda i, j: (i, 0)),
            pl.BlockSpec((1, d), lambda i, j: (0, 0)),
            pl.BlockSpec((None, 1, d), lambda i, j: (i // tpb, 0, 0)),
            pl.BlockSpec((None, 1, d), lambda i, j: (i // tpb, 0, 0)),
            pl.BlockSpec((None, d, tn), lambda i, j: (layer, 0, qk_col(j))),
            pl.BlockSpec((tn, d), lambda i, j: (v_row(j), 0)),
            pl.BlockSpec((d, LANES), lambda i, j: (0, 0)),
            pl.BlockSpec((1, LANES), lambda i, j: (0, 0)),
        ],
        out_specs=[
            pl.BlockSpec((tm, tn), lambda i, j: (i, qk_col(j))),
            pl.BlockSpec((tm // tq, tn, tq), lambda i, j: (i, v_row(j), 0)),
            pl.BlockSpec((tm, LANES), lambda i, j: (i, 0)),
        ],
        out_shape=[
            jax.ShapeDtypeStruct((n, 2 * d), BF16),
            jax.ShapeDtypeStruct((n // tq, d, tq), BF16),
            jax.ShapeDtypeStruct((n, LANES), F32),
        ],
        scratch_shapes=[pltpu.VMEM((tm, d), BF16), pltpu.VMEM((1, LANES), F32)],
        compiler_params=_cparams("arbitrary", "arbitrary"),
        name="attn_in_proj",
    )(x2, g, sh, sc, w_all, w_vt, w_f, b_f)


def _fox_kernel(q_ref, k_ref, vt_ref, fc_ref, fr_ref, o_ref, fkb_ref, va_ref, *, tq, hb):
    hg = pl.program_id(1)
    qi = pl.program_id(2)
    dh = HEAD_DIM
    nl = tq // LANES
    seq = k_ref.shape[0]
    da = dh + ONES_ROWS

    @pl.when(qi == 0)
    def _():
        lane = lax.broadcasted_iota(jnp.int32, (tq, LANES), 1)
        for c in range(seq // tq):
            rs = slice(c * tq, (c + 1) * tq)
            fc = fc_ref[rs, :]
            for hh in range(hb):
                col = jnp.sum(jnp.where(lane == hg * hb + hh, fc, 0.0), axis=1, keepdims=True)
                fkb_ref[hh, rs, :] = jnp.broadcast_to(col, (tq, LANES))
                va_ref[c, hh * da:hh * da + dh, :] = vt_ref[c, hh * dh:(hh + 1) * dh, :]
                va_ref[c, hh * da + dh:(hh + 1) * da, :] = jnp.ones((ONES_ROWS, tq), BF16)

    def step(c0, nck, carry, masked):
        tk = nck * tq
        start = pl.multiple_of(c0 * tq, tq)
        ts = []
        for hh in range(hb):
            ks = k_ref[pl.ds(start, tk), hh * dh:(hh + 1) * dh]
            q = q_ref[:, hh * dh:(hh + 1) * dh]
            fkb = fkb_ref[hh, pl.ds(start, tk), :]
            t = lax.dot_general(ks, q, (((1,), (1,)), ((), ())), preferred_element_type=F32)
            t = t - jnp.concatenate([fkb] * nl, axis=1)
            if masked:
                kpos = lax.broadcasted_iota(jnp.int32, t.shape, 0)
                qpos = lax.broadcasted_iota(jnp.int32, t.shape, 1) + (nck - 1) * tq
                t = jnp.where(kpos <= qpos, t, -jnp.inf)
            ts.append(t)
        ps = []
        for hh, (m, acc) in enumerate(carry):
            fq = fr_ref[hh, pl.ds(qi, 1), :]
            m_new = jnp.maximum(m, jnp.max(ts[hh], axis=0, keepdims=True) + fq)
            p = jnp.exp2(ts[hh] + (fq - m_new))
            alpha = jnp.exp2(m - m_new)
            ps.append((m_new, alpha, p.astype(BF16)))
        out = []
        for hh, (m, acc) in enumerate(carry):
            m_new, alpha, p = ps[hh]
            va = jnp.concatenate([va_ref[c0 + c, hh * da:(hh + 1) * da, :] for c in range(nck)],
                                 axis=1)
            out.append((m_new, alpha * acc + _mm(va, p)))
        return tuple(out)

    def finish(carry):
        for hh, (_, acc) in enumerate(carry):
            o_ref[:, hh * dh:(hh + 1) * dh] = (acc[:dh] / acc[dh:dh + 1]).T.astype(o_ref.dtype)

    init = tuple((jnp.full((1, tq), -jnp.inf, F32), jnp.zeros((da, tq), F32))
                 for _ in range(hb))
    carry = lax.fori_loop(0, qi // 2, lambda jj, c: step(2 * jj, 2, c, False), init)

    @pl.when(qi % 2 == 1)
    def _():
        finish(step(qi - 1, 2, carry, True))

    @pl.when(qi % 2 == 0)
    def _():
        finish(step(qi, 1, carry, True))


def _fox_attention(qk, vt, fcum, frow, bsz, seq, tq=512, hb=4):
    n = qk.shape[0]
    nq = seq // tq
    ng = HEADS // hb
    w = hb * HEAD_DIM
    return pl.pallas_call(
        functools.partial(_fox_kernel, tq=tq, hb=hb),
        grid=(bsz, ng, nq),
        in_specs=[
            pl.BlockSpec((tq, w), lambda b, g, i: (b * nq + i, g)),
            pl.BlockSpec((seq, w), lambda b, g, i: (b, ng + g)),
            pl.BlockSpec((nq, w, tq), lambda b, g, i: (b, g, 0)),
            pl.BlockSpec((seq, LANES), lambda b, g, i: (b, 0)),
            pl.BlockSpec((hb, nq, tq), lambda b, g, i: (b * ng + g, 0, 0)),
        ],
        out_specs=pl.BlockSpec((tq, w), lambda b, g, i: (b * nq + i, g)),
        out_shape=jax.ShapeDtypeStruct((n, HEADS * HEAD_DIM), BF16),
        scratch_shapes=[pltpu.VMEM((hb, seq, LANES), F32),
                        pltpu.VMEM((nq, hb * (HEAD_DIM + ONES_ROWS), tq), BF16)],
        compiler_params=_cparams("arbitrary", "arbitrary", "arbitrary"),
        name="fox_attention",
    )(qk, qk, vt, fcum, frow)


def _proj_res_kernel(a_ref, w_ref, x_ref, gate_ref, o_ref):
    o_ref[...] = x_ref[...] + gate_ref[...] * _mm(a_ref[...], w_ref[...])


def _proj_res(a, w, x2, gate, seq, tm=1024, tn=1024):
    n, d = x2.shape
    k = a.shape[1]
    tpb = seq // tm
    return pl.pallas_call(
        _proj_res_kernel,
        grid=(n // tm, d // tn),
        in_specs=[
            pl.BlockSpec((tm, k), lambda i, j: (i, 0)),
            pl.BlockSpec((k, tn), lambda i, j: (0, j)),
            pl.BlockSpec((tm, tn), lambda i, j: (i, j)),
            pl.BlockSpec((None, 1, tn), lambda i, j: (i // tpb, 0, j)),
        ],
        out_specs=pl.BlockSpec((tm, tn), lambda i, j: (i, j)),
        out_shape=jax.ShapeDtypeStruct((n, d), F32),
        compiler_params=_cparams("parallel", "arbitrary"),
        name="attn_out_proj",
    )(a, w, x2, gate)


def kernel(x, c, ada_w, ada_b, norm_mix_g, norm_ffn_g, a_w_in, a_b_in, a_ln_g, a_ln_b, a_w_s,
           a_b_s, a_w_out, b_w_in, b_b_f, b_w_out, ffn_w_gate, ffn_w_up, ffn_w_down, final_g):
    bsz, seq, d = x.shape
    depth = ada_w.shape[0]
    x2 = x.reshape(bsz * seq, d)

    mod = _ada_mod(c, ada_w, ada_b).reshape(depth, bsz, 6, 1, d)
    row = lambda a: a.reshape(1, -1)
    wg_bf16, wu_bf16, wd_bf16 = (w.astype(BF16) for w in (ffn_w_gate, ffn_w_up, ffn_w_down))
    b_w_in_bf16 = b_w_in.astype(BF16)

    for i in range(depth):
        sh1, sc1, g1, sh2, sc2, g2 = (mod[i, :, k] for k in range(6))
        j = i // 2
        if i % 2 == 0:
            z = _sgu_in(x2, row(norm_mix_g[i]), sh1, sc1, a_w_in[j].astype(BF16),
                        row(a_b_in[j]), seq)
            x2 = _sgu_out(z, row(a_ln_g[j]), row(a_ln_b[j]), a_w_s[j], a_b_s[j].T,
                          a_w_out[j].astype(BF16), x2, g1, seq)
        else:
            tq = 512
            w_in = b_w_in[j]
            w_vt = w_in[:, 2 * d:3 * d].T.astype(BF16)
            w_f = jnp.pad(w_in[:, 3 * d:], ((0, 0), (0, LANES - HEADS))).astype(BF16)
            b_f = jnp.pad(b_b_f[j], (0, LANES - HEADS)).reshape(1, LANES)
            qk, vt, fcum = _attn_in(x2, row(norm_mix_g[i]), sh1, sc1, b_w_in_bf16, j,
                                    w_vt, w_f, b_f, seq, tq)
            frow = fcum[:, :HEADS].reshape(bsz, seq, HEADS).transpose(0, 2, 1)
            frow = frow.reshape(bsz * HEADS, seq // tq, tq)
            o = _fox_attention(qk, vt, fcum, frow, bsz, seq, tq=tq)
            x2 = _proj_res(o, b_w_out[j].astype(BF16), x2, g1, seq)
        x2 = _ffn(x2, row(norm_ffn_g[i]), sh2, sc2, g2, wg_bf16, wu_bf16, wd_bf16, i,
                  row(final_g), seq, final_norm=(i == depth - 1))
    return x2.reshape(bsz, seq, d)
```

```python
import functools
import math

import jax
import jax.numpy as jnp
from jax import lax
from jax.experimental import pallas as pl
from jax.experimental.pallas import tpu as pltpu

F32 = jnp.float32
BF16 = jnp.bfloat16

EPS = 1e-6
CHUNK = 128
GROUPS = 16
HEADS = 16
HEAD_DIM = 128
LANES = 128
LOG2E = math.log2(math.e)
ONES_ROWS = 16
VMEM_LIMIT = 56 * 1024 * 1024


def _cparams(*sem):
    return pltpu.CompilerParams(dimension_semantics=sem, vmem_limit_bytes=VMEM_LIMIT)


def _row_chunks(tm, rc):
    return [slice(r * rc, (r + 1) * rc) for r in range(tm // rc)]


def _norm_mod(x, gm, sh):
    ms = jnp.mean(x * x, axis=-1, keepdims=True)
    return x * lax.rsqrt(ms + EPS) * gm + sh


def _mm(a, b):
    return jnp.dot(a, b, preferred_element_type=F32)


def _ada_kernel(ct_ref, w_ref, b_ref, o_ref):
    ct = ct_ref[...]
    ca = ct * jax.nn.sigmoid(ct)
    w = w_ref[...]
    rows = []
    for b in range(ct.shape[1]):
        rows.append(jnp.sum(w * ca[:, b:b + 1], axis=0, keepdims=True))
    o_ref[...] = jnp.concatenate(rows, axis=0) + b_ref[...]


def _ada_mod(c, ada_w, ada_b, tn=1024):
    depth, d, n = ada_w.shape
    bsz = c.shape[0]
    return pl.pallas_call(
        _ada_kernel,
        grid=(depth, n // tn),
        in_specs=[
            pl.BlockSpec((d, bsz), lambda l, j: (0, 0)),
            pl.BlockSpec((None, d, tn), lambda l, j: (l, 0, j)),
            pl.BlockSpec((None, 1, tn), lambda l, j: (l, 0, j)),
        ],
        out_specs=pl.BlockSpec((None, bsz, tn), lambda l, j: (l, 0, j)),
        out_shape=jax.ShapeDtypeStruct((depth, bsz, n), F32),
        compiler_params=_cparams("parallel", "parallel"),
        name="adaln_mod",
    )(c.T, ada_w, ada_b.reshape(depth, 1, n))


def _sgu_in_kernel(x_ref, g_ref, sh_ref, sc_ref, w_ref, b_ref, o_ref, h_ref, *, rc):
    j = pl.program_id(1)

    def emit(h, rs):
        z = _mm(h, w_ref[...]) + b_ref[...]
        z = 0.5 * z * (1.0 + lax.erf(z * (1.0 / math.sqrt(2.0))))
        o_ref[rs, :] = z.astype(o_ref.dtype)

    @pl.when(j == 0)
    def _():
        gm = g_ref[...] * (1.0 + sc_ref[...])
        sh = sh_ref[...]
        for rs in _row_chunks(x_ref.shape[0], rc):
            h = _norm_mod(x_ref[rs, :], gm, sh).astype(BF16)
            h_ref[rs, :] = h
            emit(h, rs)

    @pl.when(j != 0)
    def _():
        emit(h_ref[...], slice(None))


def _sgu_in(x2, g, sh, sc, w, b, seq, tm=1024, tn=1024, rc=256):
    n, d = x2.shape
    nout = w.shape[1]
    tpb = seq // tm
    return pl.pallas_call(
        functools.partial(_sgu_in_kernel, rc=rc),
        grid=(n // tm, nout // tn),
        in_specs=[
            pl.BlockSpec((tm, d), lambda i, j: (i, 0)),
            pl.BlockSpec((1, d), lambda i, j: (0, 0)),
            pl.BlockSpec((None, 1, d), lambda i, j: (i // tpb, 0, 0)),
            pl.BlockSpec((None, 1, d), lambda i, j: (i // tpb, 0, 0)),
            pl.BlockSpec((d, tn), lambda i, j: (0, j)),
            pl.BlockSpec((1, tn), lambda i, j: (0, j)),
        ],
        out_specs=pl.BlockSpec((tm, tn), lambda i, j: (i, j)),
        out_shape=jax.ShapeDtypeStruct((n, nout), BF16),
        scratch_shapes=[pltpu.VMEM((tm, d), BF16)],
        compiler_params=_cparams("parallel", "arbitrary"),
        name="sgu_in_proj",
    )(x2, g, sh, sc, w, b)


def _sgu_out_kernel(u_ref, v_ref, lg_ref, lb_ref, ws_ref, bst_ref, w_ref, x_ref, gate_ref,
                    o_ref, y_ref, wm_ref, *, rc):
    tm, aw = u_ref.shape
    j = pl.program_id(1)
    nck = rc // CHUNK

    def emit(y, rs):
        o_ref[rs, :] = x_ref[rs, :] + gate_ref[...] * _mm(y, w_ref[...])

    @pl.when(j == 0)
    def _():
        row = lax.broadcasted_iota(jnp.int32, (CHUNK, CHUNK), 0)
        col = lax.broadcasted_iota(jnp.int32, (CHUNK, CHUNK), 1)
        causal = col <= row
        for g in range(GROUPS):
            wm_ref[g] = jnp.where(causal, ws_ref[g], 0.0).astype(BF16)
        lg = lg_ref[...]
        lb = lb_ref[...]
        for rs in _row_chunks(tm, rc):
            v = v_ref[rs, :].astype(F32)
            mu = jnp.mean(v, axis=-1, keepdims=True)
            vc = v - mu
            var = jnp.mean(vc * vc, axis=-1, keepdims=True)
            vn = (vc * lax.rsqrt(var + EPS) * lg + lb).astype(BF16)
            u = u_ref[rs, :]
            for g in range(GROUPS):
                cs = slice(g * CHUNK, (g + 1) * CHUNK)
                vg = jnp.concatenate(
                    [vn[c * CHUNK:(c + 1) * CHUNK, cs] for c in range(nck)], axis=1)
                sv = _mm(wm_ref[g], vg) + bst_ref[:, g:g + 1]
                for c in range(nck):
                    r0 = rs.start + c * CHUNK
                    ug = u[c * CHUNK:(c + 1) * CHUNK, cs].astype(F32)
                    y_ref[r0:r0 + CHUNK, cs] = (
                        ug * sv[:, c * CHUNK:(c + 1) * CHUNK]).astype(BF16)
            emit(y_ref[rs, :], rs)

    @pl.when(j != 0)
    def _():
        emit(y_ref[...], slice(None))


def _sgu_out(z, ln_g, ln_b, w_s, b_s_t, w_out, x2, gate, seq, tm=512, tn=2048, rc=256):
    n, d = x2.shape
    aw = w_out.shape[0]
    tpb = seq // tm
    return pl.pallas_call(
        functools.partial(_sgu_out_kernel, rc=rc),
        grid=(n // tm, d // tn),
        in_specs=[
            pl.BlockSpec((tm, aw), lambda i, j: (i, 0)),
            pl.BlockSpec((tm, aw), lambda i, j: (i, 1)),
            pl.BlockSpec((1, aw), lambda i, j: (0, 0)),
            pl.BlockSpec((1, aw), lambda i, j: (0, 0)),
            pl.BlockSpec((GROUPS, CHUNK, CHUNK), lambda i, j: (0, 0, 0)),
            pl.BlockSpec((CHUNK, GROUPS), lambda i, j: (0, 0)),
            pl.BlockSpec((aw, tn), lambda i, j: (0, j)),
            pl.BlockSpec((tm, tn), lambda i, j: (i, j)),
            pl.BlockSpec((None, 1, tn), lambda i, j: (i // tpb, 0, j)),
        ],
        out_specs=pl.BlockSpec((tm, tn), lambda i, j: (i, j)),
        out_shape=jax.ShapeDtypeStruct((n, d), F32),
        scratch_shapes=[pltpu.VMEM((tm, aw), BF16), pltpu.VMEM((GROUPS, CHUNK, CHUNK), BF16)],
        compiler_params=_cparams("parallel", "arbitrary"),
        name="sgu_mix_out_proj",
    )(z, z, ln_g, ln_b, w_s, b_s_t, w_out, x2, gate)


def _ffn_kernel(x_ref, g_ref, sh_ref, sc_ref, gate_ref, wg_ref, wu_ref, wd_ref, fg_ref,
                o_ref, h_ref, *, final_norm, rc_edge, rc_mid):
    f = pl.program_id(1)
    last = pl.num_programs(1) - 1
    tm = x_ref.shape[0]

    def mlp(h):
        a = _mm(h, wg_ref[...])
        b = _mm(h, wu_ref[...])
        hid = (a * jax.nn.sigmoid(a) * b).astype(BF16)
        return gate_ref[...] * _mm(hid, wd_ref[...])

    @pl.when(f == 0)
    def _():
        gm = g_ref[...] * (1.0 + sc_ref[...])
        sh = sh_ref[...]
        for rs in _row_chunks(tm, rc_edge):
            x = x_ref[rs, :]
            h = _norm_mod(x, gm, sh).astype(BF16)
            h_ref[rs, :] = h
            o_ref[rs, :] = x + mlp(h)

    def accumulate(rc, normalize):
        for rs in _row_chunks(tm, rc):
            o = o_ref[rs, :] + mlp(h_ref[rs, :])
            if normalize:
                ms = jnp.mean(o * o, axis=-1, keepdims=True)
                o = o * lax.rsqrt(ms + EPS) * fg_ref[...]
            o_ref[rs, :] = o

    if final_norm:
        @pl.when((f > 0) & (f < last))
        def _():
            accumulate(rc_mid, False)

        @pl.when(f == last)
        def _():
            accumulate(rc_edge, True)
    else:
        @pl.when(f > 0)
        def _():
            accumulate(rc_mid, False)


def _ffn(x2, g, sh, sc, gate, wg, wu, wd, layer, final_g, seq, final_norm, tm=1024, tf=512,
         rc_edge=256, rc_mid=512):
    n, d = x2.shape
    fh = wg.shape[2]
    tpb = seq // tm
    vec = pl.BlockSpec((None, 1, d), lambda i, f: (i // tpb, 0, 0))
    return pl.pallas_call(
        functools.partial(_ffn_kernel, final_norm=final_norm, rc_edge=rc_edge, rc_mid=rc_mid),
        grid=(n // tm, fh // tf),
        in_specs=[
            pl.BlockSpec((tm, d), lambda i, f: (i, 0)),
            pl.BlockSpec((1, d), lambda i, f: (0, 0)),
            vec, vec, vec,
            pl.BlockSpec((None, d, tf), lambda i, f: (layer, 0, f)),
            pl.BlockSpec((None, d, tf), lambda i, f: (layer, 0, f)),
            pl.BlockSpec((None, tf, d), lambda i, f: (layer, f, 0)),
            pl.BlockSpec((1, d), lambda i, f: (0, 0)),
        ],
        out_specs=pl.BlockSpec((tm, d), lambda i, f: (i, 0)),
        out_shape=jax.ShapeDtypeStruct((n, d), F32),
        scratch_shapes=[pltpu.VMEM((tm, d), BF16)],
        compiler_params=_cparams("parallel", "arbitrary"),
        name="swiglu_ffn",
    )(x2, g, sh, sc, gate, wg, wu, wd, final_g)


def _attn_in_kernel(x_ref, g_ref, sh_ref, sc_ref, w_ref, wvt_ref, wf_ref, bf_ref,
                    o_ref, vt_ref, fc_ref, h_ref, carry_ref, *,
                    tiles_per_seq, q_tiles, qk_tiles, q_scale, rc, tq):
    i = pl.program_id(0)
    j = pl.program_id(1)
    tm = x_ref.shape[0]

    def emit(h, rs):
        z = _mm(h, w_ref[...]) * jnp.where(j < q_tiles, q_scale, 1.0)
        o_ref[rs, :] = z.astype(o_ref.dtype)

    @pl.when(j == 0)
    def _():
        @pl.when(i % tiles_per_seq == 0)
        def _():
            carry_ref[...] = jnp.zeros_like(carry_ref)

        gm = g_ref[...] * (1.0 + sc_ref[...])
        sh = sh_ref[...]
        row = lax.broadcasted_iota(jnp.int32, (rc, rc), 0)
        col = lax.broadcasted_iota(jnp.int32, (rc, rc), 1)
        tri = (col <= row).astype(BF16)
        for rs in _row_chunks(tm, rc):
            h = _norm_mod(x_ref[rs, :], gm, sh).astype(BF16)
            h_ref[rs, :] = h
            t = _mm(h, wf_ref[...]) + bf_ref[...]
            log_f = jnp.minimum(t, 0.0) - jnp.log1p(jnp.exp(-jnp.abs(t)))
            hi = log_f.astype(BF16)
            r1 = log_f - hi.astype(F32)
            mid = r1.astype(BF16)
            lo = (r1 - mid.astype(F32)).astype(BF16)
            c2 = _mm(tri, jnp.concatenate([hi, mid], axis=1))
            cum = c2[:, :LANES] + c2[:, LANES:] + _mm(tri, lo) + carry_ref[...]
            fc_ref[rs, :] = cum * LOG2E
            carry_ref[...] = cum[rc - 1:rc, :]
            emit(h, rs)

    @pl.when((j != 0) & (j < qk_tiles))
    def _():
        emit(h_ref[...], slice(None))

    @pl.when(j >= qk_tiles)
    def _():
        zt = lax.dot_general(wvt_ref[...], h_ref[...], (((1,), (1,)), ((), ())),
                             preferred_element_type=F32)
        for c in range(tm // tq):
            vt_ref[c] = zt[:, c * tq:(c + 1) * tq].astype(vt_ref.dtype)


def _attn_in(x2, g, sh, sc, w_all, layer, w_vt, w_f, b_f, seq, tq, tm=1024, tn=1024, rc=256):
    n, d = x2.shape
    tpb = seq // tm
    qk_tiles = 2 * d // tn
    v_tiles = d // tn
    kern = functools.partial(_attn_in_kernel, tiles_per_seq=tpb, q_tiles=d // tn,
                             qk_tiles=qk_tiles, q_scale=LOG2E / math.sqrt(HEAD_DIM), rc=rc, tq=tq)
    qk_col = lambda j: jnp.minimum(j, qk_tiles - 1)
    v_row = lambda j: jnp.maximum(j - qk_tiles, 0)
    return pl.pallas_call(
        kern,
        grid=(n // tm, qk_tiles + v_tiles),
        in_specs=[
            pl.BlockSpec((tm, d), lambda i, j: (i, 0)),
            pl.BlockSpec((1, d), lambda i, j: (0, 0)),
            pl.BlockSpec((None, 1, d), lambda i, j: (i // tpb, 0, 0)),
            pl.BlockSpec((None, 1, d), lambda i, j: (i // tpb, 0, 0)),
            pl.BlockSpec((None, d, tn), lambda i, j: (layer, 0, qk_col(j))),
            pl.BlockSpec((tn, d), lambda i, j: (v_row(j), 0)),
            pl.BlockSpec((d, LANES), lambda i, j: (0, 0)),
            pl.BlockSpec((1, LANES), lambda i, j: (0, 0)),
        ],
        out_specs=[
            pl.BlockSpec((tm, tn), lambda i, j: (i, qk_col(j))),
            pl.BlockSpec((tm // tq, tn, tq), lambda i, j: (i, v_row(j), 0)),
            pl.BlockSpec((tm, LANES), lambda i, j: (i, 0)),
        ],
        out_shape=[
            jax.ShapeDtypeStruct((n, 2 * d), BF16),
            jax.ShapeDtypeStruct((n // tq, d, tq), BF16),
            jax.ShapeDtypeStruct((n, LANES), F32),
        ],
        scratch_shapes=[pltpu.VMEM((tm, d), BF16), pltpu.VMEM((1, LANES), F32)],
        compiler_params=_cparams("arbitrary", "arbitrary"),
        name="attn_in_proj",
    )(x2, g, sh, sc, w_all, w_vt, w_f, b_f)


def _fox_kernel(q_ref, k_ref, vt_ref, fc_ref, fr_ref, o_ref, fkb_ref, va_ref, *, tq, hb):
    hg = pl.program_id(1)
    qi = pl.program_id(2)
    dh = HEAD_DIM
    nl = tq // LANES
    seq = k_ref.shape[0]
    da = dh + ONES_ROWS

    @pl.when(qi == 0)
    def _():
        lane = lax.broadcasted_iota(jnp.int32, (tq, LANES), 1)
        for c in range(seq // tq):
            rs = slice(c * tq, (c + 1) * tq)
            fc = fc_ref[rs, :]
            for hh in range(hb):
                col = jnp.sum(jnp.where(lane == hg * hb + hh, fc, 0.0), axis=1, keepdims=True)
                fkb_ref[hh, rs, :] = jnp.broadcast_to(col, (tq, LANES))
                va_ref[c, hh * da:hh * da + dh, :] = vt_ref[c, hh * dh:(hh + 1) * dh, :]
                va_ref[c, hh * da + dh:(hh + 1) * da, :] = jnp.ones((ONES_ROWS, tq), BF16)

    def step(c0, nck, carry, masked):
        tk = nck * tq
        start = pl.multiple_of(c0 * tq, tq)
        ts = []
        for hh in range(hb):
            ks = k_ref[pl.ds(start, tk), hh * dh:(hh + 1) * dh]
            q = q_ref[:, hh * dh:(hh + 1) * dh]
            fkb = fkb_ref[hh, pl.ds(start, tk), :]
            t = lax.dot_general(ks, q, (((1,), (1,)), ((), ())), preferred_element_type=F32)
            t = t - jnp.concatenate([fkb] * nl, axis=1)
            if masked:
                kpos = lax.broadcasted_iota(jnp.int32, t.shape, 0)
                qpos = lax.broadcasted_iota(jnp.int32, t.shape, 1) + (nck - 1) * tq
                t = jnp.where(kpos <= qpos, t, -jnp.inf)
            ts.append(t)
        ps = []
        for hh, (m, acc) in enumerate(carry):
            fq = fr_ref[hh, pl.ds(qi, 1), :]
            m_new = jnp.maximum(m, jnp.max(ts[hh], axis=0, keepdims=True) + fq)
            p = jnp.exp2(ts[hh] + (fq - m_new))
            alpha = jnp.exp2(m - m_new)
            ps.append((m_new, alpha, p.astype(BF16)))
        out = []
        for hh, (m, acc) in enumerate(carry):
            m_new, alpha, p = ps[hh]
            va = jnp.concatenate([va_ref[c0 + c, hh * da:(hh + 1) * da, :] for c in range(nck)],
                                 axis=1)
            out.append((m_new, alpha * acc + _mm(va, p)))
        return tuple(out)

    def finish(carry):
        for hh, (_, acc) in enumerate(carry):
            o_ref[:, hh * dh:(hh + 1) * dh] = (acc[:dh] / acc[dh:dh + 1]).T.astype(o_ref.dtype)

    init = tuple((jnp.full((1, tq), -jnp.inf, F32), jnp.zeros((da, tq), F32))
                 for _ in range(hb))
    carry = lax.fori_loop(0, qi // 2, lambda jj, c: step(2 * jj, 2, c, False), init)

    @pl.when(qi % 2 == 1)
    def _():
        finish(step(qi - 1, 2, carry, True))

    @pl.when(qi % 2 == 0)
    def _():
        finish(step(qi, 1, carry, True))


def _fox_attention(qk, vt, fcum, frow, bsz, seq, tq=512, hb=4):
    n = qk.shape[0]
    nq = seq // tq
    ng = HEADS // hb
    w = hb * HEAD_DIM
    return pl.pallas_call(
        functools.partial(_fox_kernel, tq=tq, hb=hb),
        grid=(bsz, ng, nq),
        in_specs=[
            pl.BlockSpec((tq, w), lambda b, g, i: (b * nq + i, g)),
            pl.BlockSpec((seq, w), lambda b, g, i: (b, ng + g)),
            pl.BlockSpec((nq, w, tq), lambda b, g, i: (b, g, 0)),
            pl.BlockSpec((seq, LANES), lambda b, g, i: (b, 0)),
            pl.BlockSpec((hb, nq, tq), lambda b, g, i: (b * ng + g, 0, 0)),
        ],
        out_specs=pl.BlockSpec((tq, w), lambda b, g, i: (b * nq + i, g)),
        out_shape=jax.ShapeDtypeStruct((n, HEADS * HEAD_DIM), BF16),
        scratch_shapes=[pltpu.VMEM((hb, seq, LANES), F32),
                        pltpu.VMEM((nq, hb * (HEAD_DIM + ONES_ROWS), tq), BF16)],
        compiler_params=_cparams("arbitrary", "arbitrary", "arbitrary"),
        name="fox_attention",
    )(qk, qk, vt, fcum, frow)


def _proj_res_kernel(a_ref, w_ref, x_ref, gate_ref, o_ref):
    o_ref[...] = x_ref[...] + gate_ref[...] * _mm(a_ref[...], w_ref[...])


def _proj_res(a, w, x2, gate, seq, tm=512, tn=2048):
    n, d = x2.shape
    k = a.shape[1]
    tpb = seq // tm
    return pl.pallas_call(
        _proj_res_kernel,
        grid=(n // tm, d // tn),
        in_specs=[
            pl.BlockSpec((tm, k), lambda i, j: (i, 0)),
            pl.BlockSpec((k, tn), lambda i, j: (0, j)),
            pl.BlockSpec((tm, tn), lambda i, j: (i, j)),
            pl.BlockSpec((None, 1, tn), lambda i, j: (i // tpb, 0, j)),
        ],
        out_specs=pl.BlockSpec((tm, tn), lambda i, j: (i, j)),
        out_shape=jax.ShapeDtypeStruct((n, d), F32),
        compiler_params=_cparams("parallel", "arbitrary"),
        name="attn_out_proj",
    )(a, w, x2, gate)


def kernel(x, c, ada_w, ada_b, norm_mix_g, norm_ffn_g, a_w_in, a_b_in, a_ln_g, a_ln_b, a_w_s,
           a_b_s, a_w_out, b_w_in, b_b_f, b_w_out, ffn_w_gate, ffn_w_up, ffn_w_down, final_g):
    bsz, seq, d = x.shape
    depth = ada_w.shape[0]
    x2 = x.reshape(bsz * seq, d)

    mod = _ada_mod(c, ada_w, ada_b).reshape(depth, bsz, 6, 1, d)
    row = lambda a: a.reshape(1, -1)
    wg_bf16, wu_bf16, wd_bf16 = (w.astype(BF16) for w in (ffn_w_gate, ffn_w_up, ffn_w_down))
    b_w_in_bf16 = b_w_in.astype(BF16)

    for i in range(depth):
        sh1, sc1, g1, sh2, sc2, g2 = (mod[i, :, k] for k in range(6))
        j = i // 2
        if i % 2 == 0:
            z = _sgu_in(x2, row(norm_mix_g[i]), sh1, sc1, a_w_in[j].astype(BF16),
                        row(a_b_in[j]), seq)
            x2 = _sgu_out(z, row(a_ln_g[j]), row(a_ln_b[j]), a_w_s[j], a_b_s[j].T,
                          a_w_out[j].astype(BF16), x2, g1, seq)
        else:
            tq = 512
            w_in = b_w_in[j]
            w_vt = w_in[:, 2 * d:3 * d].T.astype(BF16)
            w_f = jnp.pad(w_in[:, 3 * d:], ((0, 0), (0, LANES - HEADS))).astype(BF16)
            b_f = jnp.pad(b_b_f[j], (0, LANES - HEADS)).reshape(1, LANES)
            qk, vt, fcum = _attn_in(x2, row(norm_mix_g[i]), sh1, sc1, b_w_in_bf16, j,
                                    w_vt, w_f, b_f, seq, tq)
            frow = fcum[:, :HEADS].reshape(bsz, seq, HEADS).transpose(0, 2, 1)
            frow = frow.reshape(bsz * HEADS, seq // tq, tq)
            o = _fox_attention(qk, vt, fcum, frow, bsz, seq, tq=tq)
            x2 = _proj_res(o, b_w_out[j].astype(BF16), x2, g1, seq)
        x2 = _ffn(x2, row(norm_ffn_g[i]), sh2, sc2, g2, wg_bf16, wu_bf16, wd_bf16, i,
                  row(final_g), seq, final_norm=(i == depth - 1))
    return x2.reshape(bsz, seq, d)
```

```python
import functools
import math

import jax
import jax.numpy as jnp
from jax import lax
from jax.experimental import pallas as pl
from jax.experimental.pallas import tpu as pltpu

F32 = jnp.float32
BF16 = jnp.bfloat16

EPS = 1e-6
CHUNK = 128
GROUPS = 16
HEADS = 16
HEAD_DIM = 128
LANES = 128
LOG2E = math.log2(math.e)
ONES_ROWS = 16
VMEM_LIMIT = 60 * 1024 * 1024


def _cparams(*sem):
    return pltpu.CompilerParams(dimension_semantics=sem, vmem_limit_bytes=VMEM_LIMIT)


def _row_chunks(tm, rc):
    return [slice(r * rc, (r + 1) * rc) for r in range(tm // rc)]


def _norm_mod(x, gm, sh):
    ms = jnp.mean(x * x, axis=-1, keepdims=True)
    return x * lax.rsqrt(ms + EPS) * gm + sh


def _mm(a, b):
    return jnp.dot(a, b, preferred_element_type=F32)


def _ada_kernel(ct_ref, w_ref, b_ref, o_ref):
    ct = ct_ref[...]
    ca = ct * jax.nn.sigmoid(ct)
    w = w_ref[...]
    rows = []
    for b in range(ct.shape[1]):
        rows.append(jnp.sum(w * ca[:, b:b + 1], axis=0, keepdims=True))
    o_ref[...] = jnp.concatenate(rows, axis=0) + b_ref[...]


def _ada_mod(c, ada_w, ada_b, tn=1024):
    depth, d, n = ada_w.shape
    bsz = c.shape[0]
    return pl.pallas_call(
        _ada_kernel,
        grid=(depth, n // tn),
        in_specs=[
            pl.BlockSpec((d, bsz), lambda l, j: (0, 0)),
            pl.BlockSpec((None, d, tn), lambda l, j: (l, 0, j)),
            pl.BlockSpec((None, 1, tn), lambda l, j: (l, 0, j)),
        ],
        out_specs=pl.BlockSpec((None, bsz, tn), lambda l, j: (l, 0, j)),
        out_shape=jax.ShapeDtypeStruct((depth, bsz, n), F32),
        compiler_params=_cparams("parallel", "parallel"),
        name="adaln_mod",
    )(c.T, ada_w, ada_b.reshape(depth, 1, n))


def _sgu_in_kernel(x_ref, g_ref, sh_ref, sc_ref, w_ref, b_ref, o_ref, h_ref, *, rc):
    j = pl.program_id(1)

    def emit(h, rs):
        z = _mm(h, w_ref[...]) + b_ref[...]
        z = 0.5 * z * (1.0 + lax.erf(z * (1.0 / math.sqrt(2.0))))
        o_ref[rs, :] = z.astype(o_ref.dtype)

    @pl.when(j == 0)
    def _():
        gm = g_ref[...] * (1.0 + sc_ref[...])
        sh = sh_ref[...]
        for rs in _row_chunks(x_ref.shape[0], rc):
            h = _norm_mod(x_ref[rs, :], gm, sh).astype(BF16)
            h_ref[rs, :] = h
            emit(h, rs)

    @pl.when(j != 0)
    def _():
        emit(h_ref[...], slice(None))


def _sgu_in(x2, g, sh, sc, w, b, seq, tm=1024, tn=1024, rc=256):
    n, d = x2.shape
    nout = w.shape[1]
    tpb = seq // tm
    return pl.pallas_call(
        functools.partial(_sgu_in_kernel, rc=rc),
        grid=(n // tm, nout // tn),
        in_specs=[
            pl.BlockSpec((tm, d), lambda i, j: (i, 0)),
            pl.BlockSpec((1, d), lambda i, j: (0, 0)),
            pl.BlockSpec((None, 1, d), lambda i, j: (i // tpb, 0, 0)),
            pl.BlockSpec((None, 1, d), lambda i, j: (i // tpb, 0, 0)),
            pl.BlockSpec((d, tn), lambda i, j: (0, j)),
            pl.BlockSpec((1, tn), lambda i, j: (0, j)),
        ],
        out_specs=pl.BlockSpec((tm, tn), lambda i, j: (i, j)),
        out_shape=jax.ShapeDtypeStruct((n, nout), BF16),
        scratch_shapes=[pltpu.VMEM((tm, d), BF16)],
        compiler_params=_cparams("parallel", "arbitrary"),
        name="sgu_in_proj",
    )(x2, g, sh, sc, w, b)


def _sgu_out_kernel(u_ref, v_ref, lg_ref, lb_ref, ws_ref, bst_ref, w_ref, x_ref, gate_ref,
                    o_ref, y_ref, wm_ref, *, rc):
    tm, aw = u_ref.shape
    j = pl.program_id(1)
    nck = rc // CHUNK

    def emit(y, rs):
        o_ref[rs, :] = x_ref[rs, :] + gate_ref[...] * _mm(y, w_ref[...])

    @pl.when(j == 0)
    def _():
        row = lax.broadcasted_iota(jnp.int32, (CHUNK, CHUNK), 0)
        col = lax.broadcasted_iota(jnp.int32, (CHUNK, CHUNK), 1)
        causal = col <= row
        for g in range(GROUPS):
            wm_ref[g] = jnp.where(causal, ws_ref[g], 0.0).astype(BF16)
        lg = lg_ref[...]
        lb = lb_ref[...]
        for rs in _row_chunks(tm, rc):
            v = v_ref[rs, :].astype(F32)
            mu = jnp.mean(v, axis=-1, keepdims=True)
            vc = v - mu
            var = jnp.mean(vc * vc, axis=-1, keepdims=True)
            vn = (vc * lax.rsqrt(var + EPS) * lg + lb).astype(BF16)
            u = u_ref[rs, :]
            for g in range(GROUPS):
                cs = slice(g * CHUNK, (g + 1) * CHUNK)
                vg = jnp.concatenate(
                    [vn[c * CHUNK:(c + 1) * CHUNK, cs] for c in range(nck)], axis=1)
                sv = _mm(wm_ref[g], vg) + bst_ref[:, g:g + 1]
                for c in range(nck):
                    r0 = rs.start + c * CHUNK
                    ug = u[c * CHUNK:(c + 1) * CHUNK, cs].astype(F32)
                    y_ref[r0:r0 + CHUNK, cs] = (
                        ug * sv[:, c * CHUNK:(c + 1) * CHUNK]).astype(BF16)
            emit(y_ref[rs, :], rs)

    @pl.when(j != 0)
    def _():
        emit(y_ref[...], slice(None))


def _sgu_out(z, ln_g, ln_b, w_s, b_s_t, w_out, x2, gate, seq, tm=512, tn=2048, rc=256):
    n, d = x2.shape
    aw = w_out.shape[0]
    tpb = seq // tm
    return pl.pallas_call(
        functools.partial(_sgu_out_kernel, rc=rc),
        grid=(n // tm, d // tn),
        in_specs=[
            pl.BlockSpec((tm, aw), lambda i, j: (i, 0)),
            pl.BlockSpec((tm, aw), lambda i, j: (i, 1)),
            pl.BlockSpec((1, aw), lambda i, j: (0, 0)),
            pl.BlockSpec((1, aw), lambda i, j: (0, 0)),
            pl.BlockSpec((GROUPS, CHUNK, CHUNK), lambda i, j: (0, 0, 0)),
            pl.BlockSpec((CHUNK, GROUPS), lambda i, j: (0, 0)),
            pl.BlockSpec((aw, tn), lambda i, j: (0, j)),
            pl.BlockSpec((tm, tn), lambda i, j: (i, j)),
            pl.BlockSpec((None, 1, tn), lambda i, j: (i // tpb, 0, j)),
        ],
        out_specs=pl.BlockSpec((tm, tn), lambda i, j: (i, j)),
        out_shape=jax.ShapeDtypeStruct((n, d), F32),
        scratch_shapes=[pltpu.VMEM((tm, aw), BF16), pltpu.VMEM((GROUPS, CHUNK, CHUNK), BF16)],
        compiler_params=_cparams("parallel", "arbitrary"),
        name="sgu_mix_out_proj",
    )(z, z, ln_g, ln_b, w_s, b_s_t, w_out, x2, gate)


def _ffn_kernel(x_ref, g_ref, sh_ref, sc_ref, gate_ref, wg_ref, wu_ref, wd_ref, fg_ref,
                o_ref, h_ref, *, final_norm, rc_edge, rc_mid):
    f = pl.program_id(1)
    last = pl.num_programs(1) - 1
    tm = x_ref.shape[0]

    def mlp(h):
        a = _mm(h, wg_ref[...])
        b = _mm(h, wu_ref[...])
        hid = (a * jax.nn.sigmoid(a) * b).astype(BF16)
        return gate_ref[...] * _mm(hid, wd_ref[...].astype(BF16))

    @pl.when(f == 0)
    def _():
        gm = g_ref[...] * (1.0 + sc_ref[...])
        sh = sh_ref[...]
        for rs in _row_chunks(tm, rc_edge):
            x = x_ref[rs, :]
            h = _norm_mod(x, gm, sh).astype(BF16)
            h_ref[rs, :] = h
            o_ref[rs, :] = x + mlp(h)

    def accumulate(rc, normalize):
        for rs in _row_chunks(tm, rc):
            o = o_ref[rs, :] + mlp(h_ref[rs, :])
            if normalize:
                ms = jnp.mean(o * o, axis=-1, keepdims=True)
                o = o * lax.rsqrt(ms + EPS) * fg_ref[...]
            o_ref[rs, :] = o

    if final_norm:
        @pl.when((f > 0) & (f < last))
        def _():
            accumulate(rc_mid, False)

        @pl.when(f == last)
        def _():
            accumulate(rc_edge, True)
    else:
        @pl.when(f > 0)
        def _():
            accumulate(rc_mid, False)


def _ffn(x2, g, sh, sc, gate, wg, wu, wd, layer, final_g, seq, final_norm, tm=1024, tf=512,
         rc_edge=256, rc_mid=512):
    n, d = x2.shape
    fh = wg.shape[2]
    tpb = seq // tm
    vec = pl.BlockSpec((None, 1, d), lambda i, f: (i // tpb, 0, 0))
    return pl.pallas_call(
        functools.partial(_ffn_kernel, final_norm=final_norm, rc_edge=rc_edge, rc_mid=rc_mid),
        grid=(n // tm, fh // tf),
        in_specs=[
            pl.BlockSpec((tm, d), lambda i, f: (i, 0)),
            pl.BlockSpec((1, d), lambda i, f: (0, 0)),
            vec, vec, vec,
            pl.BlockSpec((None, d, tf), lambda i, f: (layer, 0, f)),
            pl.BlockSpec((None, d, tf), lambda i, f: (layer, 0, f)),
            pl.BlockSpec((None, tf, d), lambda i, f: (layer, f, 0)),
            pl.BlockSpec((1, d), lambda i, f: (0, 0)),
        ],
        out_specs=pl.BlockSpec((tm, d), lambda i, f: (i, 0)),
        out_shape=jax.ShapeDtypeStruct((n, d), F32),
        scratch_shapes=[pltpu.VMEM((tm, d), BF16)],
        compiler_params=_cparams("parallel", "arbitrary"),
        name="swiglu_ffn",
    )(x2, g, sh, sc, gate, wg, wu, wd, final_g)


def _attn_in_kernel(x_ref, g_ref, sh_ref, sc_ref, w_ref, wvt_ref, wf_ref, bf_ref,
                    o_ref, vt_ref, fc_ref, h_ref, carry_ref, *,
                    tiles_per_seq, q_tiles, qk_tiles, q_scale, rc, tq):
    i = pl.program_id(0)
    j = pl.program_id(1)
    tm = x_ref.shape[0]

    def emit(h, rs):
        z = _mm(h, w_ref[...]) * jnp.where(j < q_tiles, q_scale, 1.0)
        o_ref[rs, :] = z.astype(o_ref.dtype)

    @pl.when(j == 0)
    def _():
        @pl.when(i % tiles_per_seq == 0)
        def _():
            carry_ref[...] = jnp.zeros_like(carry_ref)

        gm = g_ref[...] * (1.0 + sc_ref[...])
        sh = sh_ref[...]
        row = lax.broadcasted_iota(jnp.int32, (rc, rc), 0)
        col = lax.broadcasted_iota(jnp.int32, (rc, rc), 1)
        tri = (col <= row).astype(BF16)
        for rs in _row_chunks(tm, rc):
            h = _norm_mod(x_ref[rs, :], gm, sh).astype(BF16)
            h_ref[rs, :] = h
            t = _mm(h, wf_ref[...]) + bf_ref[...]
            log_f = jnp.minimum(t, 0.0) - jnp.log1p(jnp.exp(-jnp.abs(t)))
            hi = log_f.astype(BF16)
            r1 = log_f - hi.astype(F32)
            mid = r1.astype(BF16)
            lo = (r1 - mid.astype(F32)).astype(BF16)
            c2 = _mm(tri, jnp.concatenate([hi, mid], axis=1))
            cum = c2[:, :LANES] + c2[:, LANES:] + _mm(tri, lo) + carry_ref[...]
            fc_ref[rs, :] = cum * LOG2E
            carry_ref[...] = cum[rc - 1:rc, :]
            emit(h, rs)

    @pl.when((j != 0) & (j < qk_tiles))
    def _():
        emit(h_ref[...], slice(None))

    @pl.when(j >= qk_tiles)
    def _():
        zt = lax.dot_general(wvt_ref[...], h_ref[...], (((1,), (1,)), ((), ())),
                             preferred_element_type=F32)
        for c in range(tm // tq):
            vt_ref[c] = zt[:, c * tq:(c + 1) * tq].astype(vt_ref.dtype)


def _attn_in(x2, g, sh, sc, w_all, layer, w_vt, w_f, b_f, seq, tq, tm=1024, tn=1024, rc=256):
    n, d = x2.shape
    tpb = seq // tm
    qk_tiles = 2 * d // tn
    v_tiles = d // tn
    kern = functools.partial(_attn_in_kernel, tiles_per_seq=tpb, q_tiles=d // tn,
                             qk_tiles=qk_tiles, q_scale=LOG2E / math.sqrt(HEAD_DIM), rc=rc, tq=tq)
    qk_col = lambda j: jnp.minimum(j, qk_tiles - 1)
    v_row = lambda j: jnp.maximum(j - qk_tiles, 0)
    return pl.pallas_call(
        kern,
        grid=(n // tm, qk_tiles + v_tiles),
        in_specs=[
            pl.BlockSpec((tm, d), lambda i, j: (i, 0)),
            pl.BlockSpec((1, d), lambda i, j: (0, 0)),
            pl.BlockSpec((None, 1, d), lambda i, j: (i // tpb, 0, 0)),
            pl.BlockSpec((None, 1, d), lambda i, j: (i // tpb, 0, 0)),
            pl.BlockSpec((None, d, tn), lambda i, j: (layer, 0, qk_col(j))),
            pl.BlockSpec((tn, d), lambda i, j: (v_row(j), 0)),
            pl.BlockSpec((d, LANES), lambda i, j: (0, 0)),
            pl.BlockSpec((1, LANES), lambda i, j: (0, 0)),
        ],
        out_specs=[
            pl.BlockSpec((tm, tn), lambda i, j: (i, qk_col(j))),
            pl.BlockSpec((tm // tq, tn, tq), lambda i, j: (i, v_row(j), 0)),
            pl.BlockSpec((tm, LANES), lambda i, j: (i, 0)),
        ],
        out_shape=[
            jax.ShapeDtypeStruct((n, 2 * d), BF16),
            jax.ShapeDtypeStruct((n // tq, d, tq), BF16),
            jax.ShapeDtypeStruct((n, LANES), F32),
        ],
        scratch_shapes=[pltpu.VMEM((tm, d), BF16), pltpu.VMEM((1, LANES), F32)],
        compiler_params=_cparams("arbitrary", "arbitrary"),
        name="attn_in_proj",
    )(x2, g, sh, sc, w_all, w_vt, w_f, b_f)


def _fox_kernel(q_ref, k_ref, vt_ref, fc_ref, fr_ref, o_ref, fkb_ref, va_ref, *, tq, hb):
    hg = pl.program_id(1)
    qi = pl.program_id(2)
    dh = HEAD_DIM
    nl = tq // LANES
    seq = k_ref.shape[0]
    da = dh + ONES_ROWS

    @pl.when(qi == 0)
    def _():
        lane = lax.broadcasted_iota(jnp.int32, (tq, LANES), 1)
        for c in range(seq // tq):
            rs = slice(c * tq, (c + 1) * tq)
            fc = fc_ref[rs, :]
            for hh in range(hb):
                col = jnp.sum(jnp.where(lane == hg * hb + hh, fc, 0.0), axis=1, keepdims=True)
                fkb_ref[hh, rs, :] = jnp.broadcast_to(col, (tq, LANES))
                va_ref[c, hh * da:hh * da + dh, :] = vt_ref[c, hh * dh:(hh + 1) * dh, :]
                va_ref[c, hh * da + dh:(hh + 1) * da, :] = jnp.ones((ONES_ROWS, tq), BF16)

    qts = [q_ref[:, hh * dh:(hh + 1) * dh].astype(F32).T.astype(BF16) for hh in range(hb)]

    def step(c0, nck, carry, masked):
        tk = nck * tq
        start = pl.multiple_of(c0 * tq, tq)
        ts = []
        for hh in range(hb):
            ks = k_ref[pl.ds(start, tk), hh * dh:(hh + 1) * dh]
            fkb = fkb_ref[hh, pl.ds(start, tk), :]
            t = _mm(ks, qts[hh]) - jnp.concatenate([fkb] * nl, axis=1)
            if masked:
                kpos = lax.broadcasted_iota(jnp.int32, t.shape, 0)
                qpos = lax.broadcasted_iota(jnp.int32, t.shape, 1) + (nck - 1) * tq
                t = jnp.where(kpos <= qpos, t, -jnp.inf)
            ts.append(t)
        ps = []
        for hh, (m, acc) in enumerate(carry):
            fq = fr_ref[hh, pl.ds(qi, 1), :]
            m_new = jnp.maximum(m, jnp.max(ts[hh], axis=0, keepdims=True) + fq)
            p = jnp.exp2(ts[hh] + (fq - m_new))
            alpha = jnp.exp2(m - m_new)
            ps.append((m_new, alpha, p.astype(BF16)))
        out = []
        for hh, (m, acc) in enumerate(carry):
            m_new, alpha, p = ps[hh]
            va = jnp.concatenate([va_ref[c0 + c, hh * da:(hh + 1) * da, :] for c in range(nck)],
                                 axis=1)
            out.append((m_new, alpha * acc + _mm(va, p)))
        return tuple(out)

    def finish(carry):
        for hh, (_, acc) in enumerate(carry):
            o_ref[:, hh * dh:(hh + 1) * dh] = (acc[:dh] / acc[dh:dh + 1]).T.astype(o_ref.dtype)

    init = tuple((jnp.full((1, tq), -jnp.inf, F32), jnp.zeros((da, tq), F32))
                 for _ in range(hb))
    carry = lax.fori_loop(0, qi // 2, lambda jj, c: step(2 * jj, 2, c, False), init)

    @pl.when(qi % 2 == 1)
    def _():
        finish(step(qi - 1, 2, carry, True))

    @pl.when(qi % 2 == 0)
    def _():
        finish(step(qi, 1, carry, True))


def _fox_attention(qk, vt, fcum, frow, bsz, seq, tq=512, hb=4):
    n = qk.shape[0]
    nq = seq // tq
    ng = HEADS // hb
    w = hb * HEAD_DIM
    return pl.pallas_call(
        functools.partial(_fox_kernel, tq=tq, hb=hb),
        grid=(bsz, ng, nq),
        in_specs=[
            pl.BlockSpec((tq, w), lambda b, g, i: (b * nq + i, g)),
            pl.BlockSpec((seq, w), lambda b, g, i: (b, ng + g)),
            pl.BlockSpec((nq, w, tq), lambda b, g, i: (b, g, 0)),
            pl.BlockSpec((seq, LANES), lambda b, g, i: (b, 0)),
            pl.BlockSpec((hb, nq, tq), lambda b, g, i: (b * ng + g, 0, 0)),
        ],
        out_specs=pl.BlockSpec((tq, w), lambda b, g, i: (b * nq + i, g)),
        out_shape=jax.ShapeDtypeStruct((n, HEADS * HEAD_DIM), BF16),
        scratch_shapes=[pltpu.VMEM((hb, seq, LANES), F32),
                        pltpu.VMEM((nq, hb * (HEAD_DIM + ONES_ROWS), tq), BF16)],
        compiler_params=_cparams("arbitrary", "arbitrary", "arbitrary"),
        name="fox_attention",
    )(qk, qk, vt, fcum, frow)


def _proj_res_kernel(a_ref, w_ref, x_ref, gate_ref, o_ref):
    o_ref[...] = x_ref[...] + gate_ref[...] * _mm(a_ref[...], w_ref[...])


def _proj_res(a, w, x2, gate, seq, tm=512, tn=2048):
    n, d = x2.shape
    k = a.shape[1]
    tpb = seq // tm
    return pl.pallas_call(
        _proj_res_kernel,
        grid=(n // tm, d // tn),
        in_specs=[
            pl.BlockSpec((tm, k), lambda i, j: (i, 0)),
            pl.BlockSpec((k, tn), lambda i, j: (0, j)),
            pl.BlockSpec((tm, tn), lambda i, j: (i, j)),
            pl.BlockSpec((None, 1, tn), lambda i, j: (i // tpb, 0, j)),
        ],
        out_specs=pl.BlockSpec((tm, tn), lambda i, j: (i, j)),
        out_shape=jax.ShapeDtypeStruct((n, d), F32),
        compiler_params=_cparams("parallel", "arbitrary"),
        name="attn_out_proj",
    )(a, w, x2, gate)


def kernel(x, c, ada_w, ada_b, norm_mix_g, norm_ffn_g, a_w_in, a_b_in, a_ln_g, a_ln_b, a_w_s,
           a_b_s, a_w_out, b_w_in, b_b_f, b_w_out, ffn_w_gate, ffn_w_up, ffn_w_down, final_g):
    bsz, seq, d = x.shape
    depth = ada_w.shape[0]
    x2 = x.reshape(bsz * seq, d)

    mod = _ada_mod(c, ada_w, ada_b).reshape(depth, bsz, 6, 1, d)
    row = lambda a: a.reshape(1, -1)
    wg_bf16, wu_bf16 = ffn_w_gate.astype(BF16), ffn_w_up.astype(BF16)
    b_w_in_bf16 = b_w_in.astype(BF16)

    for i in range(depth):
        sh1, sc1, g1, sh2, sc2, g2 = (mod[i, :, k] for k in range(6))
        j = i // 2
        if i % 2 == 0:
            z = _sgu_in(x2, row(norm_mix_g[i]), sh1, sc1, a_w_in[j].astype(BF16),
                        row(a_b_in[j]), seq)
            x2 = _sgu_out(z, row(a_ln_g[j]), row(a_ln_b[j]), a_w_s[j], a_b_s[j].T,
                          a_w_out[j].astype(BF16), x2, g1, seq)
        else:
            tq = 512
            w_in = b_w_in[j]
            w_vt = w_in[:, 2 * d:3 * d].T.astype(BF16)
            w_f = jnp.pad(w_in[:, 3 * d:], ((0, 0), (0, LANES - HEADS))).astype(BF16)
            b_f = jnp.pad(b_b_f[j], (0, LANES - HEADS)).reshape(1, LANES)
            qk, vt, fcum = _attn_in(x2, row(norm_mix_g[i]), sh1, sc1, b_w_in_bf16, j,
                                    w_vt, w_f, b_f, seq, tq)
            frow = fcum[:, :HEADS].reshape(bsz, seq, HEADS).transpose(0, 2, 1)
            frow = frow.reshape(bsz * HEADS, seq // tq, tq)
            o = _fox_attention(qk, vt, fcum, frow, bsz, seq, tq=tq)
            x2 = _proj_res(o, b_w_out[j].astype(BF16), x2, g1, seq)
        x2 = _ffn(x2, row(norm_ffn_g[i]), sh2, sc2, g2, wg_bf16, wu_bf16, ffn_w_down, i,
                  row(final_g), seq, final_norm=(i == depth - 1))
    return x2.reshape(bsz, seq, d)
```

```python
import functools
import math

import jax
import jax.numpy as jnp
from jax import lax
from jax.experimental import pallas as pl
from jax.experimental.pallas import tpu as pltpu

F32 = jnp.float32
BF16 = jnp.bfloat16

EPS = 1e-6
CHUNK = 128
GROUPS = 16
HEADS = 16
HEAD_DIM = 128
LANES = 128
LOG2E = math.log2(math.e)
ONES_ROWS = 16
MAX_LAZY_EXP = 60.0
VMEM_LIMIT = 56 * 1024 * 1024


def _cparams(*sem):
    return pltpu.CompilerParams(dimension_semantics=sem, vmem_limit_bytes=VMEM_LIMIT)


def _row_chunks(tm, rc):
    return [slice(r * rc, (r + 1) * rc) for r in range(tm // rc)]


def _norm_mod(x, gm, sh):
    ms = jnp.mean(x * x, axis=-1, keepdims=True)
    return x * lax.rsqrt(ms + EPS) * gm + sh


def _mm(a, b):
    return jnp.dot(a, b, preferred_element_type=F32)


def _ada_kernel(ct_ref, w_ref, b_ref, o_ref):
    ct = ct_ref[...]
    ca = ct * jax.nn.sigmoid(ct)
    w = w_ref[...]
    rows = []
    for b in range(ct.shape[1]):
        rows.append(jnp.sum(w * ca[:, b:b + 1], axis=0, keepdims=True))
    o_ref[...] = jnp.concatenate(rows, axis=0) + b_ref[...]


def _ada_mod(c, ada_w, ada_b, tn=1024):
    depth, d, n = ada_w.shape
    bsz = c.shape[0]
    return pl.pallas_call(
        _ada_kernel,
        grid=(depth, n // tn),
        in_specs=[
            pl.BlockSpec((d, bsz), lambda l, j: (0, 0)),
            pl.BlockSpec((None, d, tn), lambda l, j: (l, 0, j)),
            pl.BlockSpec((None, 1, tn), lambda l, j: (l, 0, j)),
        ],
        out_specs=pl.BlockSpec((None, bsz, tn), lambda l, j: (l, 0, j)),
        out_shape=jax.ShapeDtypeStruct((depth, bsz, n), F32),
        compiler_params=_cparams("parallel", "parallel"),
        name="adaln_mod",
    )(c.T, ada_w, ada_b.reshape(depth, 1, n))


def _sgu_in_kernel(x_ref, g_ref, sh_ref, sc_ref, w_ref, b_ref, o_ref, h_ref, *, rc):
    j = pl.program_id(1)

    def emit(h, rs):
        z = _mm(h, w_ref[...]) + b_ref[...]
        z = 0.5 * z * (1.0 + lax.erf(z * (1.0 / math.sqrt(2.0))))
        o_ref[rs, :] = z.astype(o_ref.dtype)

    @pl.when(j == 0)
    def _():
        gm = g_ref[...] * (1.0 + sc_ref[...])
        sh = sh_ref[...]
        for rs in _row_chunks(x_ref.shape[0], rc):
            h = _norm_mod(x_ref[rs, :], gm, sh).astype(BF16)
            h_ref[rs, :] = h
            emit(h, rs)

    @pl.when(j != 0)
    def _():
        emit(h_ref[...], slice(None))


def _sgu_in(x2, g, sh, sc, w, b, seq, tm=1024, tn=1024, rc=256):
    n, d = x2.shape
    nout = w.shape[1]
    tpb = seq // tm
    return pl.pallas_call(
        functools.partial(_sgu_in_kernel, rc=rc),
        grid=(n // tm, nout // tn),
        in_specs=[
            pl.BlockSpec((tm, d), lambda i, j: (i, 0)),
            pl.BlockSpec((1, d), lambda i, j: (0, 0)),
            pl.BlockSpec((None, 1, d), lambda i, j: (i // tpb, 0, 0)),
            pl.BlockSpec((None, 1, d), lambda i, j: (i // tpb, 0, 0)),
            pl.BlockSpec((d, tn), lambda i, j: (0, j)),
            pl.BlockSpec((1, tn), lambda i, j: (0, j)),
        ],
        out_specs=pl.BlockSpec((tm, tn), lambda i, j: (i, j)),
        out_shape=jax.ShapeDtypeStruct((n, nout), BF16),
        scratch_shapes=[pltpu.VMEM((tm, d), BF16)],
        compiler_params=_cparams("parallel", "arbitrary"),
        name="sgu_in_proj",
    )(x2, g, sh, sc, w, b)


def _sgu_out_kernel(u_ref, v_ref, lg_ref, lb_ref, ws_ref, bst_ref, w_ref, x_ref, gate_ref,
                    o_ref, y_ref, wm_ref, *, rc):
    tm, aw = u_ref.shape
    j = pl.program_id(1)
    nck = rc // CHUNK

    def emit(y, rs):
        o_ref[rs, :] = x_ref[rs, :] + gate_ref[...] * _mm(y, w_ref[...])

    @pl.when(j == 0)
    def _():
        row = lax.broadcasted_iota(jnp.int32, (CHUNK, CHUNK), 0)
        col = lax.broadcasted_iota(jnp.int32, (CHUNK, CHUNK), 1)
        causal = col <= row
        for g in range(GROUPS):
            wm_ref[g] = jnp.where(causal, ws_ref[g], 0.0).astype(BF16)
        lg = lg_ref[...]
        lb = lb_ref[...]
        for rs in _row_chunks(tm, rc):
            v = v_ref[rs, :].astype(F32)
            mu = jnp.mean(v, axis=-1, keepdims=True)
            vc = v - mu
            var = jnp.mean(vc * vc, axis=-1, keepdims=True)
            vn = (vc * lax.rsqrt(var + EPS) * lg + lb).astype(BF16)
            u = u_ref[rs, :]
            for g in range(GROUPS):
                cs = slice(g * CHUNK, (g + 1) * CHUNK)
                vg = jnp.concatenate(
                    [vn[c * CHUNK:(c + 1) * CHUNK, cs] for c in range(nck)], axis=1)
                sv = _mm(wm_ref[g], vg) + bst_ref[:, g:g + 1]
                for c in range(nck):
                    r0 = rs.start + c * CHUNK
                    ug = u[c * CHUNK:(c + 1) * CHUNK, cs].astype(F32)
                    y_ref[r0:r0 + CHUNK, cs] = (
                        ug * sv[:, c * CHUNK:(c + 1) * CHUNK]).astype(BF16)
            emit(y_ref[rs, :], rs)

    @pl.when(j != 0)
    def _():
        emit(y_ref[...], slice(None))


def _sgu_out(z, ln_g, ln_b, w_s, b_s_t, w_out, x2, gate, seq, tm=512, tn=2048, rc=256):
    n, d = x2.shape
    aw = w_out.shape[0]
    tpb = seq // tm
    return pl.pallas_call(
        functools.partial(_sgu_out_kernel, rc=rc),
        grid=(n // tm, d // tn),
        in_specs=[
            pl.BlockSpec((tm, aw), lambda i, j: (i, 0)),
            pl.BlockSpec((tm, aw), lambda i, j: (i, 1)),
            pl.BlockSpec((1, aw), lambda i, j: (0, 0)),
            pl.BlockSpec((1, aw), lambda i, j: (0, 0)),
            pl.BlockSpec((GROUPS, CHUNK, CHUNK), lambda i, j: (0, 0, 0)),
            pl.BlockSpec((CHUNK, GROUPS), lambda i, j: (0, 0)),
            pl.BlockSpec((aw, tn), lambda i, j: (0, j)),
            pl.BlockSpec((tm, tn), lambda i, j: (i, j)),
            pl.BlockSpec((None, 1, tn), lambda i, j: (i // tpb, 0, j)),
        ],
        out_specs=pl.BlockSpec((tm, tn), lambda i, j: (i, j)),
        out_shape=jax.ShapeDtypeStruct((n, d), F32),
        scratch_shapes=[pltpu.VMEM((tm, aw), BF16), pltpu.VMEM((GROUPS, CHUNK, CHUNK), BF16)],
        compiler_params=_cparams("parallel", "arbitrary"),
        name="sgu_mix_out_proj",
    )(z, z, ln_g, ln_b, w_s, b_s_t, w_out, x2, gate)


def _ffn_kernel(x_ref, g_ref, sh_ref, sc_ref, gate_ref, wg_ref, wu_ref, wd_ref, fg_ref,
                o_ref, h_ref, *, final_norm, rc_edge, rc_mid):
    f = pl.program_id(1)
    last = pl.num_programs(1) - 1
    tm = x_ref.shape[0]

    def mlp(h):
        a = _mm(h, wg_ref[...])
        b = _mm(h, wu_ref[...])
        hid = (a * jax.nn.sigmoid(a) * b).astype(BF16)
        return gate_ref[...] * _mm(hid, wd_ref[...])

    @pl.when(f == 0)
    def _():
        gm = g_ref[...] * (1.0 + sc_ref[...])
        sh = sh_ref[...]
        for rs in _row_chunks(tm, rc_edge):
            x = x_ref[rs, :]
            h = _norm_mod(x, gm, sh).astype(BF16)
            h_ref[rs, :] = h
            o_ref[rs, :] = x + mlp(h)

    def accumulate(rc, normalize):
        for rs in _row_chunks(tm, rc):
            o = o_ref[rs, :] + mlp(h_ref[rs, :])
            if normalize:
                ms = jnp.mean(o * o, axis=-1, keepdims=True)
                o = o * lax.rsqrt(ms + EPS) * fg_ref[...]
            o_ref[rs, :] = o

    if final_norm:
        @pl.when((f > 0) & (f < last))
        def _():
            accumulate(rc_mid, False)

        @pl.when(f == last)
        def _():
            accumulate(rc_edge, True)
    else:
        @pl.when(f > 0)
        def _():
            accumulate(rc_mid, False)


def _ffn(x2, g, sh, sc, gate, wg, wu, wd, layer, final_g, seq, final_norm, tm=1024, tf=512,
         rc_edge=256, rc_mid=512):
    n, d = x2.shape
    fh = wg.shape[2]
    tpb = seq // tm
    vec = pl.BlockSpec((None, 1, d), lambda i, f: (i // tpb, 0, 0))
    return pl.pallas_call(
        functools.partial(_ffn_kernel, final_norm=final_norm, rc_edge=rc_edge, rc_mid=rc_mid),
        grid=(n // tm, fh // tf),
        in_specs=[
            pl.BlockSpec((tm, d), lambda i, f: (i, 0)),
            pl.BlockSpec((1, d), lambda i, f: (0, 0)),
            vec, vec, vec,
            pl.BlockSpec((None, d, tf), lambda i, f: (layer, 0, f)),
            pl.BlockSpec((None, d, tf), lambda i, f: (layer, 0, f)),
            pl.BlockSpec((None, tf, d), lambda i, f: (layer, f, 0)),
            pl.BlockSpec((1, d), lambda i, f: (0, 0)),
        ],
        out_specs=pl.BlockSpec((tm, d), lambda i, f: (i, 0)),
        out_shape=jax.ShapeDtypeStruct((n, d), F32),
        scratch_shapes=[pltpu.VMEM((tm, d), BF16)],
        compiler_params=_cparams("parallel", "arbitrary"),
        name="swiglu_ffn",
    )(x2, g, sh, sc, gate, wg, wu, wd, final_g)


def _attn_in_kernel(x_ref, g_ref, sh_ref, sc_ref, w_ref, wvt_ref, wf_ref, bf_ref,
                    o_ref, vt_ref, fc_ref, h_ref, carry_ref, *,
                    tiles_per_seq, q_tiles, qk_tiles, q_scale, rc, tq):
    i = pl.program_id(0)
    j = pl.program_id(1)
    tm = x_ref.shape[0]

    def emit(h, rs):
        z = _mm(h, w_ref[...]) * jnp.where(j < q_tiles, q_scale, 1.0)
        o_ref[rs, :] = z.astype(o_ref.dtype)

    @pl.when(j == 0)
    def _():
        @pl.when(i % tiles_per_seq == 0)
        def _():
            carry_ref[...] = jnp.zeros_like(carry_ref)

        gm = g_ref[...] * (1.0 + sc_ref[...])
        sh = sh_ref[...]
        row = lax.broadcasted_iota(jnp.int32, (rc, rc), 0)
        col = lax.broadcasted_iota(jnp.int32, (rc, rc), 1)
        tri = (col <= row).astype(BF16)
        for rs in _row_chunks(tm, rc):
            h = _norm_mod(x_ref[rs, :], gm, sh).astype(BF16)
            h_ref[rs, :] = h
            t = _mm(h, wf_ref[...]) + bf_ref[...]
            log_f = jnp.minimum(t, 0.0) - jnp.log1p(jnp.exp(-jnp.abs(t)))
            hi = log_f.astype(BF16)
            r1 = log_f - hi.astype(F32)
            mid = r1.astype(BF16)
            lo = (r1 - mid.astype(F32)).astype(BF16)
            c2 = _mm(tri, jnp.concatenate([hi, mid], axis=1))
            cum = c2[:, :LANES] + c2[:, LANES:] + _mm(tri, lo) + carry_ref[...]
            fc_ref[rs, :] = cum * LOG2E
            carry_ref[...] = cum[rc - 1:rc, :]
            emit(h, rs)

    @pl.when((j != 0) & (j < qk_tiles))
    def _():
        emit(h_ref[...], slice(None))

    @pl.when(j >= qk_tiles)
    def _():
        zt = lax.dot_general(wvt_ref[...], h_ref[...], (((1,), (1,)), ((), ())),
                             preferred_element_type=F32)
        for c in range(tm // tq):
            vt_ref[c] = zt[:, c * tq:(c + 1) * tq].astype(vt_ref.dtype)


def _attn_in(x2, g, sh, sc, w_all, layer, w_vt, w_f, b_f, seq, tq, tm=1024, tn=1024, rc=256):
    n, d = x2.shape
    tpb = seq // tm
    qk_tiles = 2 * d // tn
    v_tiles = d // tn
    kern = functools.partial(_attn_in_kernel, tiles_per_seq=tpb, q_tiles=d // tn,
                             qk_tiles=qk_tiles, q_scale=LOG2E / math.sqrt(HEAD_DIM), rc=rc, tq=tq)
    qk_col = lambda j: jnp.minimum(j, qk_tiles - 1)
    v_row = lambda j: jnp.maximum(j - qk_tiles, 0)
    return pl.pallas_call(
        kern,
        grid=(n // tm, qk_tiles + v_tiles),
        in_specs=[
            pl.BlockSpec((tm, d), lambda i, j: (i, 0)),
            pl.BlockSpec((1, d), lambda i, j: (0, 0)),
            pl.BlockSpec((None, 1, d), lambda i, j: (i // tpb, 0, 0)),
            pl.BlockSpec((None, 1, d), lambda i, j: (i // tpb, 0, 0)),
            pl.BlockSpec((None, d, tn), lambda i, j: (layer, 0, qk_col(j))),
            pl.BlockSpec((tn, d), lambda i, j: (v_row(j), 0)),
            pl.BlockSpec((d, LANES), lambda i, j: (0, 0)),
            pl.BlockSpec((1, LANES), lambda i, j: (0, 0)),
        ],
        out_specs=[
            pl.BlockSpec((tm, tn), lambda i, j: (i, qk_col(j))),
            pl.BlockSpec((tm // tq, tn, tq), lambda i, j: (i, v_row(j), 0)),
            pl.BlockSpec((tm, LANES), lambda i, j: (i, 0)),
        ],
        out_shape=[
            jax.ShapeDtypeStruct((n, 2 * d), BF16),
            jax.ShapeDtypeStruct((n // tq, d, tq), BF16),
            jax.ShapeDtypeStruct((n, LANES), F32),
        ],
        scratch_shapes=[pltpu.VMEM((tm, d), BF16), pltpu.VMEM((1, LANES), F32)],
        compiler_params=_cparams("arbitrary", "arbitrary"),
        name="attn_in_proj",
    )(x2, g, sh, sc, w_all, w_vt, w_f, b_f)


def _fox_kernel(q_ref, k_ref, vt_ref, fc_ref, fr_ref, o_ref, fkb_ref, va_ref, *, tq, hb):
    hg = pl.program_id(1)
    qi = pl.program_id(2)
    dh = HEAD_DIM
    nl = tq // LANES
    seq = k_ref.shape[0]
    da = dh + ONES_ROWS

    @pl.when(qi == 0)
    def _():
        lane = lax.broadcasted_iota(jnp.int32, (tq, LANES), 1)
        for c in range(seq // tq):
            rs = slice(c * tq, (c + 1) * tq)
            fc = fc_ref[rs, :]
            for hh in range(hb):
                col = jnp.sum(jnp.where(lane == hg * hb + hh, fc, 0.0), axis=1, keepdims=True)
                fkb_ref[hh, rs, :] = jnp.broadcast_to(col, (tq, LANES))
                va_ref[c, hh * da:hh * da + dh, :] = vt_ref[c, hh * dh:(hh + 1) * dh, :]
                va_ref[c, hh * da + dh:(hh + 1) * da, :] = jnp.ones((ONES_ROWS, tq), BF16)

    def step(c0, nck, carry, masked):
        tk = nck * tq
        start = pl.multiple_of(c0 * tq, tq)
        ts = []
        for hh in range(hb):
            ks = k_ref[pl.ds(start, tk), hh * dh:(hh + 1) * dh]
            q = q_ref[:, hh * dh:(hh + 1) * dh]
            fkb = fkb_ref[hh, pl.ds(start, tk), :]
            t = lax.dot_general(ks, q, (((1,), (1,)), ((), ())), preferred_element_type=F32)
            t = t - jnp.concatenate([fkb] * nl, axis=1)
            if masked:
                kpos = lax.broadcasted_iota(jnp.int32, t.shape, 0)
                qpos = lax.broadcasted_iota(jnp.int32, t.shape, 1) + (nck - 1) * tq
                t = jnp.where(kpos <= qpos, t, -jnp.inf)
            ts.append(t)
        ps = []
        for hh, (m, acc) in enumerate(carry):
            fq = fr_ref[hh, pl.ds(qi, 1), :]
            m_new = jnp.maximum(m, jnp.max(ts[hh], axis=0, keepdims=True) + fq)
            p = jnp.exp2(ts[hh] + (fq - m_new))
            alpha = jnp.exp2(m - m_new)
            ps.append((m_new, alpha, p.astype(BF16)))
        out = []
        for hh, (m, acc) in enumerate(carry):
            m_new, alpha, p = ps[hh]
            va = jnp.concatenate([va_ref[c0 + c, hh * da:(hh + 1) * da, :] for c in range(nck)],
                                 axis=1)
            out.append((m_new, alpha * acc + _mm(va, p)))
        return tuple(out)

    def lazy_step(c0, nck, state):
        carry, ovf = state
        tk = nck * tq
        start = pl.multiple_of(c0 * tq, tq)
        ts = []
        for hh in range(hb):
            ks = k_ref[pl.ds(start, tk), hh * dh:(hh + 1) * dh]
            q = q_ref[:, hh * dh:(hh + 1) * dh]
            fkb = fkb_ref[hh, pl.ds(start, tk), :]
            t = lax.dot_general(ks, q, (((1,), (1,)), ((), ())), preferred_element_type=F32)
            ts.append(t - jnp.concatenate([fkb] * nl, axis=1))
        ps = []
        for hh, (m, acc) in enumerate(carry):
            fq = fr_ref[hh, pl.ds(qi, 1), :]
            p = jnp.exp2(ts[hh] + (fq - m)).astype(BF16)
            bmax = jnp.max(ts[hh], axis=0, keepdims=True) + fq
            m_new = jnp.maximum(m, bmax)
            ovf = jnp.maximum(ovf, bmax - m)
            ps.append((m_new, jnp.exp2(m - m_new), p))
        out = []
        for hh, (m, acc) in enumerate(carry):
            m_new, alpha, p = ps[hh]
            va = jnp.concatenate([va_ref[c0 + c, hh * da:(hh + 1) * da, :] for c in range(nck)],
                                 axis=1)
            out.append((m_new, alpha * (acc + _mm(va, p))))
        return tuple(out), ovf

    def finish(carry):
        for hh, (_, acc) in enumerate(carry):
            o_ref[:, hh * dh:(hh + 1) * dh] = (acc[:dh] / acc[dh:dh + 1]).T.astype(o_ref.dtype)

    init = tuple((jnp.full((1, tq), -jnp.inf, F32), jnp.zeros((da, tq), F32))
                 for _ in range(hb))
    state = (step(qi, 1, init, True), jnp.full((1, tq), -jnp.inf, F32))
    state = lax.fori_loop(0, qi // 2, lambda jj, st: lazy_step(2 * jj, 2, st), state)
    state = lax.cond(qi % 2 == 1, lambda st: lazy_step(qi - 1, 1, st), lambda st: st, state)
    carry, ovf = state
    safe = jnp.max(ovf) <= MAX_LAZY_EXP

    @pl.when(safe)
    def _():
        finish(carry)

    @pl.when(jnp.logical_not(safe))
    def _():
        c = lax.fori_loop(0, qi, lambda j, cc: step(j, 1, cc, False), init)
        finish(step(qi, 1, c, True))


def _fox_attention(qk, vt, fcum, frow, bsz, seq, tq=512, hb=4):
    n = qk.shape[0]
    nq = seq // tq
    ng = HEADS // hb
    w = hb * HEAD_DIM
    return pl.pallas_call(
        functools.partial(_fox_kernel, tq=tq, hb=hb),
        grid=(bsz, ng, nq),
        in_specs=[
            pl.BlockSpec((tq, w), lambda b, g, i: (b * nq + i, g)),
            pl.BlockSpec((seq, w), lambda b, g, i: (b, ng + g)),
            pl.BlockSpec((nq, w, tq), lambda b, g, i: (b, g, 0)),
            pl.BlockSpec((seq, LANES), lambda b, g, i: (b, 0)),
            pl.BlockSpec((hb, nq, tq), lambda b, g, i: (b * ng + g, 0, 0)),
        ],
        out_specs=pl.BlockSpec((tq, w), lambda b, g, i: (b * nq + i, g)),
        out_shape=jax.ShapeDtypeStruct((n, HEADS * HEAD_DIM), BF16),
        scratch_shapes=[pltpu.VMEM((hb, seq, LANES), F32),
                        pltpu.VMEM((nq, hb * (HEAD_DIM + ONES_ROWS), tq), BF16)],
        compiler_params=_cparams("arbitrary", "arbitrary", "arbitrary"),
        name="fox_attention",
    )(qk, qk, vt, fcum, frow)


def _proj_res_kernel(a_ref, w_ref, x_ref, gate_ref, o_ref):
    o_ref[...] = x_ref[...] + gate_ref[...] * _mm(a_ref[...], w_ref[...])


def _proj_res(a, w, x2, gate, seq, tm=512, tn=2048):
    n, d = x2.shape
    k = a.shape[1]
    tpb = seq // tm
    return pl.pallas_call(
        _proj_res_kernel,
        grid=(n // tm, d // tn),
        in_specs=[
            pl.BlockSpec((tm, k), lambda i, j: (i, 0)),
            pl.BlockSpec((k, tn), lambda i, j: (0, j)),
            pl.BlockSpec((tm, tn), lambda i, j: (i, j)),
            pl.BlockSpec((None, 1, tn), lambda i, j: (i // tpb, 0, j)),
        ],
        out_specs=pl.BlockSpec((tm, tn), lambda i, j: (i, j)),
        out_shape=jax.ShapeDtypeStruct((n, d), F32),
        compiler_params=_cparams("parallel", "arbitrary"),
        name="attn_out_proj",
    )(a, w, x2, gate)


def kernel(x, c, ada_w, ada_b, norm_mix_g, norm_ffn_g, a_w_in, a_b_in, a_ln_g, a_ln_b, a_w_s,
           a_b_s, a_w_out, b_w_in, b_b_f, b_w_out, ffn_w_gate, ffn_w_up, ffn_w_down, final_g):
    bsz, seq, d = x.shape
    depth = ada_w.shape[0]
    x2 = x.reshape(bsz * seq, d)

    mod = _ada_mod(c, ada_w, ada_b).reshape(depth, bsz, 6, 1, d)
    row = lambda a: a.reshape(1, -1)
    wg_bf16, wu_bf16, wd_bf16 = (w.astype(BF16) for w in (ffn_w_gate, ffn_w_up, ffn_w_down))
    b_w_in_bf16 = b_w_in.astype(BF16)

    for i in range(depth):
        sh1, sc1, g1, sh2, sc2, g2 = (mod[i, :, k] for k in range(6))
        j = i // 2
        if i % 2 == 0:
            z = _sgu_in(x2, row(norm_mix_g[i]), sh1, sc1, a_w_in[j].astype(BF16),
                        row(a_b_in[j]), seq)
            x2 = _sgu_out(z, row(a_ln_g[j]), row(a_ln_b[j]), a_w_s[j], a_b_s[j].T,
                          a_w_out[j].astype(BF16), x2, g1, seq)
        else:
            tq = 512
            w_in = b_w_in[j]
            w_vt = w_in[:, 2 * d:3 * d].T.astype(BF16)
            w_f = jnp.pad(w_in[:, 3 * d:], ((0, 0), (0, LANES - HEADS))).astype(BF16)
            b_f = jnp.pad(b_b_f[j], (0, LANES - HEADS)).reshape(1, LANES)
            qk, vt, fcum = _attn_in(x2, row(norm_mix_g[i]), sh1, sc1, b_w_in_bf16, j,
                                    w_vt, w_f, b_f, seq, tq)
            frow = fcum[:, :HEADS].reshape(bsz, seq, HEADS).transpose(0, 2, 1)
            frow = frow.reshape(bsz * HEADS, seq // tq, tq)
            o = _fox_attention(qk, vt, fcum, frow, bsz, seq, tq=tq)
            x2 = _proj_res(o, b_w_out[j].astype(BF16), x2, g1, seq)
        x2 = _ffn(x2, row(norm_ffn_g[i]), sh2, sc2, g2, wg_bf16, wu_bf16, wd_bf16, i,
                  row(final_g), seq, final_norm=(i == depth - 1))
    return x2.reshape(bsz, seq, d)
```

```python
import functools
import math

import jax
import jax.numpy as jnp
from jax import lax
from jax.experimental import pallas as pl
from jax.experimental.pallas import tpu as pltpu

F32 = jnp.float32
BF16 = jnp.bfloat16

EPS = 1e-6
CHUNK = 128
GROUPS = 16
HEADS = 16
HEAD_DIM = 128
LANES = 128
LOG2E = math.log2(math.e)
ONES_ROWS = 16
MAX_LAZY_EXP = 60.0
VMEM_LIMIT = 56 * 1024 * 1024


def _cparams(*sem):
    return pltpu.CompilerParams(dimension_semantics=sem, vmem_limit_bytes=VMEM_LIMIT)


def _row_chunks(tm, rc):
    return [slice(r * rc, (r + 1) * rc) for r in range(tm // rc)]


def _norm_mod(x, gm, sh):
    ms = jnp.mean(x * x, axis=-1, keepdims=True)
    return x * lax.rsqrt(ms + EPS) * gm + sh


def _mm(a, b):
    return jnp.dot(a, b, preferred_element_type=F32)


def _ada_kernel(ct_ref, w_ref, b_ref, o_ref):
    ct = ct_ref[...]
    ca = ct * jax.nn.sigmoid(ct)
    w = w_ref[...]
    rows = []
    for b in range(ct.shape[1]):
        rows.append(jnp.sum(w * ca[:, b:b + 1], axis=0, keepdims=True))
    o_ref[...] = jnp.concatenate(rows, axis=0) + b_ref[...]


def _ada_mod(c, ada_w, ada_b, tn=1024):
    depth, d, n = ada_w.shape
    bsz = c.shape[0]
    return pl.pallas_call(
        _ada_kernel,
        grid=(depth, n // tn),
        in_specs=[
            pl.BlockSpec((d, bsz), lambda l, j: (0, 0)),
            pl.BlockSpec((None, d, tn), lambda l, j: (l, 0, j)),
            pl.BlockSpec((None, 1, tn), lambda l, j: (l, 0, j)),
        ],
        out_specs=pl.BlockSpec((None, bsz, tn), lambda l, j: (l, 0, j)),
        out_shape=jax.ShapeDtypeStruct((depth, bsz, n), F32),
        compiler_params=_cparams("parallel", "parallel"),
        name="adaln_mod",
    )(c.T, ada_w, ada_b.reshape(depth, 1, n))


def _sgu_in_kernel(x_ref, g_ref, sh_ref, sc_ref, w_ref, b_ref, o_ref, h_ref, *, rc):
    j = pl.program_id(1)

    def emit(h, rs):
        z = _mm(h, w_ref[...]) + b_ref[...]
        z = 0.5 * z * (1.0 + lax.erf(z * (1.0 / math.sqrt(2.0))))
        o_ref[rs, :] = z.astype(o_ref.dtype)

    @pl.when(j == 0)
    def _():
        gm = g_ref[...] * (1.0 + sc_ref[...])
        sh = sh_ref[...]
        for rs in _row_chunks(x_ref.shape[0], rc):
            h = _norm_mod(x_ref[rs, :], gm, sh).astype(BF16)
            h_ref[rs, :] = h
            emit(h, rs)

    @pl.when(j != 0)
    def _():
        emit(h_ref[...], slice(None))


def _sgu_in(x2, g, sh, sc, w, b, seq, tm=1024, tn=1024, rc=256):
    n, d = x2.shape
    nout = w.shape[1]
    tpb = seq // tm
    return pl.pallas_call(
        functools.partial(_sgu_in_kernel, rc=rc),
        grid=(n // tm, nout // tn),
        in_specs=[
            pl.BlockSpec((tm, d), lambda i, j: (i, 0)),
            pl.BlockSpec((1, d), lambda i, j: (0, 0)),
            pl.BlockSpec((None, 1, d), lambda i, j: (i // tpb, 0, 0)),
            pl.BlockSpec((None, 1, d), lambda i, j: (i // tpb, 0, 0)),
            pl.BlockSpec((d, tn), lambda i, j: (0, j)),
            pl.BlockSpec((1, tn), lambda i, j: (0, j)),
        ],
        out_specs=pl.BlockSpec((tm, tn), lambda i, j: (i, j)),
        out_shape=jax.ShapeDtypeStruct((n, nout), BF16),
        scratch_shapes=[pltpu.VMEM((tm, d), BF16)],
        compiler_params=_cparams("parallel", "arbitrary"),
        name="sgu_in_proj",
    )(x2, g, sh, sc, w, b)


def _sgu_out_kernel(u_ref, v_ref, lg_ref, lb_ref, ws_ref, bst_ref, w_ref, x_ref, gate_ref,
                    o_ref, y_ref, wm_ref, *, rc):
    tm, aw = u_ref.shape
    j = pl.program_id(1)
    nck = rc // CHUNK

    def emit(y, rs):
        o_ref[rs, :] = x_ref[rs, :] + gate_ref[...] * _mm(y, w_ref[...])

    @pl.when(j == 0)
    def _():
        row = lax.broadcasted_iota(jnp.int32, (CHUNK, CHUNK), 0)
        col = lax.broadcasted_iota(jnp.int32, (CHUNK, CHUNK), 1)
        causal = col <= row
        for g in range(GROUPS):
            wm_ref[g] = jnp.where(causal, ws_ref[g], 0.0).astype(BF16)
        lg = lg_ref[...]
        lb = lb_ref[...]
        for rs in _row_chunks(tm, rc):
            v = v_ref[rs, :].astype(F32)
            mu = jnp.mean(v, axis=-1, keepdims=True)
            vc = v - mu
            var = jnp.mean(vc * vc, axis=-1, keepdims=True)
            vn = (vc * lax.rsqrt(var + EPS) * lg + lb).astype(BF16)
            u = u_ref[rs, :]
            for g in range(GROUPS):
                cs = slice(g * CHUNK, (g + 1) * CHUNK)
                vg = jnp.concatenate(
                    [vn[c * CHUNK:(c + 1) * CHUNK, cs] for c in range(nck)], axis=1)
                sv = _mm(wm_ref[g], vg) + bst_ref[:, g:g + 1]
                for c in range(nck):
                    r0 = rs.start + c * CHUNK
                    ug = u[c * CHUNK:(c + 1) * CHUNK, cs].astype(F32)
                    y_ref[r0:r0 + CHUNK, cs] = (
                        ug * sv[:, c * CHUNK:(c + 1) * CHUNK]).astype(BF16)
            emit(y_ref[rs, :], rs)

    @pl.when(j != 0)
    def _():
        emit(y_ref[...], slice(None))


def _sgu_out(z, ln_g, ln_b, w_s, b_s_t, w_out, x2, gate, seq, tm=512, tn=2048, rc=256):
    n, d = x2.shape
    aw = w_out.shape[0]
    tpb = seq // tm
    return pl.pallas_call(
        functools.partial(_sgu_out_kernel, rc=rc),
        grid=(n // tm, d // tn),
        in_specs=[
            pl.BlockSpec((tm, aw), lambda i, j: (i, 0)),
            pl.BlockSpec((tm, aw), lambda i, j: (i, 1)),
            pl.BlockSpec((1, aw), lambda i, j: (0, 0)),
            pl.BlockSpec((1, aw), lambda i, j: (0, 0)),
            pl.BlockSpec((GROUPS, CHUNK, CHUNK), lambda i, j: (0, 0, 0)),
            pl.BlockSpec((CHUNK, GROUPS), lambda i, j: (0, 0)),
            pl.BlockSpec((aw, tn), lambda i, j: (0, j)),
            pl.BlockSpec((tm, tn), lambda i, j: (i, j)),
            pl.BlockSpec((None, 1, tn), lambda i, j: (i // tpb, 0, j)),
        ],
        out_specs=pl.BlockSpec((tm, tn), lambda i, j: (i, j)),
        out_shape=jax.ShapeDtypeStruct((n, d), F32),
        scratch_shapes=[pltpu.VMEM((tm, aw), BF16), pltpu.VMEM((GROUPS, CHUNK, CHUNK), BF16)],
        compiler_params=_cparams("parallel", "arbitrary"),
        name="sgu_mix_out_proj",
    )(z, z, ln_g, ln_b, w_s, b_s_t, w_out, x2, gate)


def _ffn_kernel(x_ref, g_ref, sh_ref, sc_ref, gate_ref, wg_ref, wu_ref, wd_ref, fg_ref,
                o_ref, h_ref, *, final_norm, rc_edge, rc_mid):
    f = pl.program_id(1)
    last = pl.num_programs(1) - 1
    tm = x_ref.shape[0]

    def mlp(h):
        a = _mm(h, wg_ref[...])
        b = _mm(h, wu_ref[...])
        hid = (a * jax.nn.sigmoid(a) * b).astype(BF16)
        return gate_ref[...] * _mm(hid, wd_ref[...])

    @pl.when(f == 0)
    def _():
        gm = g_ref[...] * (1.0 + sc_ref[...])
        sh = sh_ref[...]
        for rs in _row_chunks(tm, rc_edge):
            x = x_ref[rs, :]
            h = _norm_mod(x, gm, sh).astype(BF16)
            h_ref[rs, :] = h
            o_ref[rs, :] = x + mlp(h)

    def accumulate(rc, normalize):
        for rs in _row_chunks(tm, rc):
            o = o_ref[rs, :] + mlp(h_ref[rs, :])
            if normalize:
                ms = jnp.mean(o * o, axis=-1, keepdims=True)
                o = o * lax.rsqrt(ms + EPS) * fg_ref[...]
            o_ref[rs, :] = o

    if final_norm:
        @pl.when((f > 0) & (f < last))
        def _():
            accumulate(rc_mid, False)

        @pl.when(f == last)
        def _():
            accumulate(rc_edge, True)
    else:
        @pl.when(f > 0)
        def _():
            accumulate(rc_mid, False)


def _ffn(x2, g, sh, sc, gate, wg, wu, wd, layer, final_g, seq, final_norm, tm=1024, tf=512,
         rc_edge=256, rc_mid=512):
    n, d = x2.shape
    fh = wg.shape[2]
    tpb = seq // tm
    vec = pl.BlockSpec((None, 1, d), lambda i, f: (i // tpb, 0, 0))
    return pl.pallas_call(
        functools.partial(_ffn_kernel, final_norm=final_norm, rc_edge=rc_edge, rc_mid=rc_mid),
        grid=(n // tm, fh // tf),
        in_specs=[
            pl.BlockSpec((tm, d), lambda i, f: (i, 0)),
            pl.BlockSpec((1, d), lambda i, f: (0, 0)),
            vec, vec, vec,
            pl.BlockSpec((None, d, tf), lambda i, f: (layer, 0, f)),
            pl.BlockSpec((None, d, tf), lambda i, f: (layer, 0, f)),
            pl.BlockSpec((None, tf, d), lambda i, f: (layer, f, 0)),
            pl.BlockSpec((1, d), lambda i, f: (0, 0)),
        ],
        out_specs=pl.BlockSpec((tm, d), lambda i, f: (i, 0)),
        out_shape=jax.ShapeDtypeStruct((n, d), F32),
        scratch_shapes=[pltpu.VMEM((tm, d), BF16)],
        compiler_params=_cparams("parallel", "arbitrary"),
        name="swiglu_ffn",
    )(x2, g, sh, sc, gate, wg, wu, wd, final_g)


def _attn_in_kernel(x_ref, g_ref, sh_ref, sc_ref, w_ref, wvt_ref, wf_ref, bf_ref,
                    o_ref, vt_ref, fc_ref, h_ref, carry_ref, *,
                    tiles_per_seq, q_tiles, qk_tiles, q_scale, rc, tq):
    i = pl.program_id(0)
    j = pl.program_id(1)
    tm = x_ref.shape[0]

    def emit(h, rs):
        z = _mm(h, w_ref[...]) * jnp.where(j < q_tiles, q_scale, 1.0)
        o_ref[rs, :] = z.astype(o_ref.dtype)

    @pl.when(j == 0)
    def _():
        @pl.when(i % tiles_per_seq == 0)
        def _():
            carry_ref[...] = jnp.zeros_like(carry_ref)

        gm = g_ref[...] * (1.0 + sc_ref[...])
        sh = sh_ref[...]
        row = lax.broadcasted_iota(jnp.int32, (rc, rc), 0)
        col = lax.broadcasted_iota(jnp.int32, (rc, rc), 1)
        tri = (col <= row).astype(BF16)
        for rs in _row_chunks(tm, rc):
            h = _norm_mod(x_ref[rs, :], gm, sh).astype(BF16)
            h_ref[rs, :] = h
            t = _mm(h, wf_ref[...]) + bf_ref[...]
            log_f = jnp.minimum(t, 0.0) - jnp.log1p(jnp.exp(-jnp.abs(t)))
            hi = log_f.astype(BF16)
            r1 = log_f - hi.astype(F32)
            mid = r1.astype(BF16)
            lo = (r1 - mid.astype(F32)).astype(BF16)
            c2 = _mm(tri, jnp.concatenate([hi, mid], axis=1))
            cum = c2[:, :LANES] + c2[:, LANES:] + _mm(tri, lo) + carry_ref[...]
            fc_ref[rs, :] = cum * LOG2E
            carry_ref[...] = cum[rc - 1:rc, :]
            emit(h, rs)

    @pl.when((j != 0) & (j < qk_tiles))
    def _():
        emit(h_ref[...], slice(None))

    @pl.when(j >= qk_tiles)
    def _():
        zt = lax.dot_general(wvt_ref[...], h_ref[...], (((1,), (1,)), ((), ())),
                             preferred_element_type=F32)
        for c in range(tm // tq):
            vt_ref[c] = zt[:, c * tq:(c + 1) * tq].astype(vt_ref.dtype)


def _attn_in(x2, g, sh, sc, w_all, layer, w_vt, w_f, b_f, seq, tq, tm=1024, tn=1024, rc=256):
    n, d = x2.shape
    tpb = seq // tm
    qk_tiles = 2 * d // tn
    v_tiles = d // tn
    kern = functools.partial(_attn_in_kernel, tiles_per_seq=tpb, q_tiles=d // tn,
                             qk_tiles=qk_tiles, q_scale=LOG2E / math.sqrt(HEAD_DIM), rc=rc, tq=tq)
    qk_col = lambda j: jnp.minimum(j, qk_tiles - 1)
    v_row = lambda j: jnp.maximum(j - qk_tiles, 0)
    return pl.pallas_call(
        kern,
        grid=(n // tm, qk_tiles + v_tiles),
        in_specs=[
            pl.BlockSpec((tm, d), lambda i, j: (i, 0)),
            pl.BlockSpec((1, d), lambda i, j: (0, 0)),
            pl.BlockSpec((None, 1, d), lambda i, j: (i // tpb, 0, 0)),
            pl.BlockSpec((None, 1, d), lambda i, j: (i // tpb, 0, 0)),
            pl.BlockSpec((None, d, tn), lambda i, j: (layer, 0, qk_col(j))),
            pl.BlockSpec((tn, d), lambda i, j: (v_row(j), 0)),
            pl.BlockSpec((d, LANES), lambda i, j: (0, 0)),
            pl.BlockSpec((1, LANES), lambda i, j: (0, 0)),
        ],
        out_specs=[
            pl.BlockSpec((tm, tn), lambda i, j: (i, qk_col(j))),
            pl.BlockSpec((tm // tq, tn, tq), lambda i, j: (i, v_row(j), 0)),
            pl.BlockSpec((tm, LANES), lambda i, j: (i, 0)),
        ],
        out_shape=[
            jax.ShapeDtypeStruct((n, 2 * d), BF16),
            jax.ShapeDtypeStruct((n // tq, d, tq), BF16),
            jax.ShapeDtypeStruct((n, LANES), F32),
        ],
        scratch_shapes=[pltpu.VMEM((tm, d), BF16), pltpu.VMEM((1, LANES), F32)],
        compiler_params=_cparams("arbitrary", "arbitrary"),
        name="attn_in_proj",
    )(x2, g, sh, sc, w_all, w_vt, w_f, b_f)


def _fox_kernel(q_ref, k_ref, vt_ref, fc_ref, fr_ref, o_ref, fkb_ref, va_ref, ka_ref, *, tq, hb):
    hg = pl.program_id(1)
    qi = pl.program_id(2)
    dh = HEAD_DIM
    nl = tq // LANES
    seq = k_ref.shape[0]
    da = dh + ONES_ROWS

    @pl.when(qi == 0)
    def _():
        lane = lax.broadcasted_iota(jnp.int32, (tq, LANES), 1)
        for c in range(seq // tq):
            rs = slice(c * tq, (c + 1) * tq)
            fc = fc_ref[rs, :]
            for hh in range(hb):
                col = jnp.sum(jnp.where(lane == hg * hb + hh, fc, 0.0), axis=1, keepdims=True)
                fkb_ref[hh, rs, :] = jnp.broadcast_to(col, (tq, LANES))
                hi = col.astype(BF16).astype(F32)
                r1 = col - hi
                mid = r1.astype(BF16).astype(F32)
                ext = jnp.where(lane == 0, hi, jnp.where(lane == 1, mid,
                                                         jnp.where(lane == 2, r1 - mid, 0.0)))
                ka_ref[hh, rs, :dh] = k_ref[rs, hh * dh:(hh + 1) * dh]
                ka_ref[hh, rs, dh:] = ext.astype(BF16)
                va_ref[c, hh * da:hh * da + dh, :] = vt_ref[c, hh * dh:(hh + 1) * dh, :]
                va_ref[c, hh * da + dh:(hh + 1) * da, :] = jnp.ones((ONES_ROWS, tq), BF16)

    def step(c0, nck, carry, masked):
        tk = nck * tq
        start = pl.multiple_of(c0 * tq, tq)
        ts = []
        for hh in range(hb):
            ks = k_ref[pl.ds(start, tk), hh * dh:(hh + 1) * dh]
            q = q_ref[:, hh * dh:(hh + 1) * dh]
            fkb = fkb_ref[hh, pl.ds(start, tk), :]
            t = lax.dot_general(ks, q, (((1,), (1,)), ((), ())), preferred_element_type=F32)
            t = t - jnp.concatenate([fkb] * nl, axis=1)
            if masked:
                kpos = lax.broadcasted_iota(jnp.int32, t.shape, 0)
                qpos = lax.broadcasted_iota(jnp.int32, t.shape, 1) + (nck - 1) * tq
                t = jnp.where(kpos <= qpos, t, -jnp.inf)
            ts.append(t)
        ps = []
        for hh, (m, acc) in enumerate(carry):
            fq = fr_ref[hh, pl.ds(qi, 1), :]
            m_new = jnp.maximum(m, jnp.max(ts[hh], axis=0, keepdims=True) + fq)
            p = jnp.exp2(ts[hh] + (fq - m_new))
            alpha = jnp.exp2(m - m_new)
            ps.append((m_new, alpha, p.astype(BF16)))
        out = []
        for hh, (m, acc) in enumerate(carry):
            m_new, alpha, p = ps[hh]
            va = jnp.concatenate([va_ref[c0 + c, hh * da:(hh + 1) * da, :] for c in range(nck)],
                                 axis=1)
            out.append((m_new, alpha * acc + _mm(va, p)))
        return tuple(out)

    lane_q = lax.broadcasted_iota(jnp.int32, (tq, LANES), 1)
    neg = jnp.where(lane_q < 3, -1.0, 0.0).astype(BF16)
    qa = [jnp.concatenate([q_ref[:, hh * dh:(hh + 1) * dh], neg], axis=1) for hh in range(hb)]

    def lazy_step(c0, nck, state):
        carry, ovf = state
        tk = nck * tq
        start = pl.multiple_of(c0 * tq, tq)
        ts = []
        for hh in range(hb):
            ks = ka_ref[hh, pl.ds(start, tk), :]
            ts.append(lax.dot_general(ks, qa[hh], (((1,), (1,)), ((), ())),
                                      preferred_element_type=F32))
        ps = []
        for hh, (m, acc) in enumerate(carry):
            fq = fr_ref[hh, pl.ds(qi, 1), :]
            p = jnp.exp2(ts[hh] + (fq - m)).astype(BF16)
            bmax = jnp.max(ts[hh], axis=0, keepdims=True) + fq
            m_new = jnp.maximum(m, bmax)
            ovf = jnp.maximum(ovf, bmax - m)
            ps.append((m_new, jnp.exp2(m - m_new), p))
        out = []
        for hh, (m, acc) in enumerate(carry):
            m_new, alpha, p = ps[hh]
            va = jnp.concatenate([va_ref[c0 + c, hh * da:(hh + 1) * da, :] for c in range(nck)],
                                 axis=1)
            out.append((m_new, alpha * (acc + _mm(va, p))))
        return tuple(out), ovf

    def finish(carry):
        for hh, (_, acc) in enumerate(carry):
            o_ref[:, hh * dh:(hh + 1) * dh] = (acc[:dh] / acc[dh:dh + 1]).T.astype(o_ref.dtype)

    init = tuple((jnp.full((1, tq), -jnp.inf, F32), jnp.zeros((da, tq), F32))
                 for _ in range(hb))
    state = (step(qi, 1, init, True), jnp.full((1, tq), -jnp.inf, F32))
    state = lax.fori_loop(0, qi // 2, lambda jj, st: lazy_step(2 * jj, 2, st), state)
    state = lax.cond(qi % 2 == 1, lambda st: lazy_step(qi - 1, 1, st), lambda st: st, state)
    carry, ovf = state
    safe = jnp.max(ovf) <= MAX_LAZY_EXP

    @pl.when(safe)
    def _():
        finish(carry)

    @pl.when(jnp.logical_not(safe))
    def _():
        c = lax.fori_loop(0, qi, lambda j, cc: step(j, 1, cc, False), init)
        finish(step(qi, 1, c, True))


def _fox_attention(qk, vt, fcum, frow, bsz, seq, tq=512, hb=4):
    n = qk.shape[0]
    nq = seq // tq
    ng = HEADS // hb
    w = hb * HEAD_DIM
    return pl.pallas_call(
        functools.partial(_fox_kernel, tq=tq, hb=hb),
        grid=(bsz, ng, nq),
        in_specs=[
            pl.BlockSpec((tq, w), lambda b, g, i: (b * nq + i, g)),
            pl.BlockSpec((seq, w), lambda b, g, i: (b, ng + g)),
            pl.BlockSpec((nq, w, tq), lambda b, g, i: (b, g, 0)),
            pl.BlockSpec((seq, LANES), lambda b, g, i: (b, 0)),
            pl.BlockSpec((hb, nq, tq), lambda b, g, i: (b * ng + g, 0, 0)),
        ],
        out_specs=pl.BlockSpec((tq, w), lambda b, g, i: (b * nq + i, g)),
        out_shape=jax.ShapeDtypeStruct((n, HEADS * HEAD_DIM), BF16),
        scratch_shapes=[pltpu.VMEM((hb, seq, LANES), F32),
                        pltpu.VMEM((nq, hb * (HEAD_DIM + ONES_ROWS), tq), BF16),
                        pltpu.VMEM((hb, seq, 2 * HEAD_DIM), BF16)],
        compiler_params=_cparams("arbitrary", "arbitrary", "arbitrary"),
        name="fox_attention",
    )(qk, qk, vt, fcum, frow)


def _proj_res_kernel(a_ref, w_ref, x_ref, gate_ref, o_ref):
    o_ref[...] = x_ref[...] + gate_ref[...] * _mm(a_ref[...], w_ref[...])


def _proj_res(a, w, x2, gate, seq, tm=512, tn=2048):
    n, d = x2.shape
    k = a.shape[1]
    tpb = seq // tm
    return pl.pallas_call(
        _proj_res_kernel,
        grid=(n // tm, d // tn),
        in_specs=[
            pl.BlockSpec((tm, k), lambda i, j: (i, 0)),
            pl.BlockSpec((k, tn), lambda i, j: (0, j)),
            pl.BlockSpec((tm, tn), lambda i, j: (i, j)),
            pl.BlockSpec((None, 1, tn), lambda i, j: (i // tpb, 0, j)),
        ],
        out_specs=pl.BlockSpec((tm, tn), lambda i, j: (i, j)),
        out_shape=jax.ShapeDtypeStruct((n, d), F32),
        compiler_params=_cparams("parallel", "arbitrary"),
        name="attn_out_proj",
    )(a, w, x2, gate)


def kernel(x, c, ada_w, ada_b, norm_mix_g, norm_ffn_g, a_w_in, a_b_in, a_ln_g, a_ln_b, a_w_s,
           a_b_s, a_w_out, b_w_in, b_b_f, b_w_out, ffn_w_gate, ffn_w_up, ffn_w_down, final_g):
    bsz, seq, d = x.shape
    depth = ada_w.shape[0]
    x2 = x.reshape(bsz * seq, d)

    mod = _ada_mod(c, ada_w, ada_b).reshape(depth, bsz, 6, 1, d)
    row = lambda a: a.reshape(1, -1)
    wg_bf16, wu_bf16, wd_bf16 = (w.astype(BF16) for w in (ffn_w_gate, ffn_w_up, ffn_w_down))
    b_w_in_bf16 = b_w_in.astype(BF16)

    for i in range(depth):
        sh1, sc1, g1, sh2, sc2, g2 = (mod[i, :, k] for k in range(6))
        j = i // 2
        if i % 2 == 0:
            z = _sgu_in(x2, row(norm_mix_g[i]), sh1, sc1, a_w_in[j].astype(BF16),
                        row(a_b_in[j]), seq)
            x2 = _sgu_out(z, row(a_ln_g[j]), row(a_ln_b[j]), a_w_s[j], a_b_s[j].T,
                          a_w_out[j].astype(BF16), x2, g1, seq)
        else:
            tq = 512
            w_in = b_w_in[j]
            w_vt = w_in[:, 2 * d:3 * d].T.astype(BF16)
            w_f = jnp.pad(w_in[:, 3 * d:], ((0, 0), (0, LANES - HEADS))).astype(BF16)
            b_f = jnp.pad(b_b_f[j], (0, LANES - HEADS)).reshape(1, LANES)
            qk, vt, fcum = _attn_in(x2, row(norm_mix_g[i]), sh1, sc1, b_w_in_bf16, j,
                                    w_vt, w_f, b_f, seq, tq)
            frow = fcum[:, :HEADS].reshape(bsz, seq, HEADS).transpose(0, 2, 1)
            frow = frow.reshape(bsz * HEADS, seq // tq, tq)
            o = _fox_attention(qk, vt, fcum, frow, bsz, seq, tq=tq)
            x2 = _proj_res(o, b_w_out[j].astype(BF16), x2, g1, seq)
        x2 = _ffn(x2, row(norm_ffn_g[i]), sh2, sc2, g2, wg_bf16, wu_bf16, wd_bf16, i,
                  row(final_g), seq, final_norm=(i == depth - 1))
    return x2.reshape(bsz, seq, d)
```

```python
import functools
import math

import jax
import jax.numpy as jnp
from jax import lax
from jax.experimental import pallas as pl
from jax.experimental.pallas import tpu as pltpu

F32 = jnp.float32
BF16 = jnp.bfloat16

EPS = 1e-6
CHUNK = 128
GROUPS = 16
HEADS = 16
HEAD_DIM = 128
LANES = 128
LOG2E = math.log2(math.e)
ONES_ROWS = 16
MAX_LAZY_EXP = 60.0
VMEM_LIMIT = 56 * 1024 * 1024


def _cparams(*sem):
    return pltpu.CompilerParams(dimension_semantics=sem, vmem_limit_bytes=VMEM_LIMIT)


def _row_chunks(tm, rc):
    return [slice(r * rc, (r + 1) * rc) for r in range(tm // rc)]


def _norm_mod(x, gm, sh):
    ms = jnp.mean(x * x, axis=-1, keepdims=True)
    return x * lax.rsqrt(ms + EPS) * gm + sh


def _mm(a, b):
    return jnp.dot(a, b, preferred_element_type=F32)


def _ada_kernel(ct_ref, w_ref, b_ref, o_ref):
    ct = ct_ref[...]
    ca = ct * jax.nn.sigmoid(ct)
    w = w_ref[...]
    rows = []
    for b in range(ct.shape[1]):
        rows.append(jnp.sum(w * ca[:, b:b + 1], axis=0, keepdims=True))
    o_ref[...] = jnp.concatenate(rows, axis=0) + b_ref[...]


def _ada_mod(c, ada_w, ada_b, tn=1024):
    depth, d, n = ada_w.shape
    bsz = c.shape[0]
    return pl.pallas_call(
        _ada_kernel,
        grid=(depth, n // tn),
        in_specs=[
            pl.BlockSpec((d, bsz), lambda l, j: (0, 0)),
            pl.BlockSpec((None, d, tn), lambda l, j: (l, 0, j)),
            pl.BlockSpec((None, 1, tn), lambda l, j: (l, 0, j)),
        ],
        out_specs=pl.BlockSpec((None, bsz, tn), lambda l, j: (l, 0, j)),
        out_shape=jax.ShapeDtypeStruct((depth, bsz, n), F32),
        compiler_params=_cparams("parallel", "parallel"),
        name="adaln_mod",
    )(c.T, ada_w, ada_b.reshape(depth, 1, n))


def _sgu_in_kernel(x_ref, g_ref, sh_ref, sc_ref, w_ref, b_ref, o_ref, h_ref, *, rc):
    j = pl.program_id(1)

    def emit(h, rs):
        z = _mm(h, w_ref[...]) + b_ref[...]
        z = 0.5 * z * (1.0 + lax.erf(z * (1.0 / math.sqrt(2.0))))
        o_ref[rs, :] = z.astype(o_ref.dtype)

    @pl.when(j == 0)
    def _():
        gm = g_ref[...] * (1.0 + sc_ref[...])
        sh = sh_ref[...]
        for rs in _row_chunks(x_ref.shape[0], rc):
            h = _norm_mod(x_ref[rs, :], gm, sh).astype(BF16)
            h_ref[rs, :] = h
            emit(h, rs)

    @pl.when(j != 0)
    def _():
        emit(h_ref[...], slice(None))


def _sgu_in(x2, g, sh, sc, w, b, seq, tm=1024, tn=1024, rc=256):
    n, d = x2.shape
    nout = w.shape[1]
    tpb = seq // tm
    return pl.pallas_call(
        functools.partial(_sgu_in_kernel, rc=rc),
        grid=(n // tm, nout // tn),
        in_specs=[
            pl.BlockSpec((tm, d), lambda i, j: (i, 0)),
            pl.BlockSpec((1, d), lambda i, j: (0, 0)),
            pl.BlockSpec((None, 1, d), lambda i, j: (i // tpb, 0, 0)),
            pl.BlockSpec((None, 1, d), lambda i, j: (i // tpb, 0, 0)),
            pl.BlockSpec((d, tn), lambda i, j: (0, j)),
            pl.BlockSpec((1, tn), lambda i, j: (0, j)),
        ],
        out_specs=pl.BlockSpec((tm, tn), lambda i, j: (i, j)),
        out_shape=jax.ShapeDtypeStruct((n, nout), BF16),
        scratch_shapes=[pltpu.VMEM((tm, d), BF16)],
        compiler_params=_cparams("parallel", "arbitrary"),
        name="sgu_in_proj",
    )(x2, g, sh, sc, w, b)


def _sgu_out_kernel(u_ref, v_ref, lg_ref, lb_ref, ws_ref, bst_ref, w_ref, x_ref, gate_ref,
                    o_ref, y_ref, wm_ref, *, rc):
    tm, aw = u_ref.shape
    j = pl.program_id(1)
    nck = rc // CHUNK

    def emit(y, rs):
        o_ref[rs, :] = x_ref[rs, :] + gate_ref[...] * _mm(y, w_ref[...])

    @pl.when(j == 0)
    def _():
        row = lax.broadcasted_iota(jnp.int32, (CHUNK, CHUNK), 0)
        col = lax.broadcasted_iota(jnp.int32, (CHUNK, CHUNK), 1)
        causal = col <= row
        for g in range(GROUPS):
            wm_ref[g] = jnp.where(causal, ws_ref[g], 0.0).astype(BF16)
        lg = lg_ref[...]
        lb = lb_ref[...]
        for rs in _row_chunks(tm, rc):
            v = v_ref[rs, :].astype(F32)
            mu = jnp.mean(v, axis=-1, keepdims=True)
            vc = v - mu
            var = jnp.mean(vc * vc, axis=-1, keepdims=True)
            vn = (vc * lax.rsqrt(var + EPS) * lg + lb).astype(BF16)
            u = u_ref[rs, :]
            for g in range(GROUPS):
                cs = slice(g * CHUNK, (g + 1) * CHUNK)
                vg = jnp.concatenate(
                    [vn[c * CHUNK:(c + 1) * CHUNK, cs] for c in range(nck)], axis=1)
                sv = _mm(wm_ref[g], vg) + bst_ref[:, g:g + 1]
                for c in range(nck):
                    r0 = rs.start + c * CHUNK
                    ug = u[c * CHUNK:(c + 1) * CHUNK, cs].astype(F32)
                    y_ref[r0:r0 + CHUNK, cs] = (
                        ug * sv[:, c * CHUNK:(c + 1) * CHUNK]).astype(BF16)
            emit(y_ref[rs, :], rs)

    @pl.when(j != 0)
    def _():
        emit(y_ref[...], slice(None))


def _sgu_out(z, ln_g, ln_b, w_s, b_s_t, w_out, x2, gate, seq, tm=512, tn=2048, rc=256):
    n, d = x2.shape
    aw = w_out.shape[0]
    tpb = seq // tm
    return pl.pallas_call(
        functools.partial(_sgu_out_kernel, rc=rc),
        grid=(n // tm, d // tn),
        in_specs=[
            pl.BlockSpec((tm, aw), lambda i, j: (i, 0)),
            pl.BlockSpec((tm, aw), lambda i, j: (i, 1)),
            pl.BlockSpec((1, aw), lambda i, j: (0, 0)),
            pl.BlockSpec((1, aw), lambda i, j: (0, 0)),
            pl.BlockSpec((GROUPS, CHUNK, CHUNK), lambda i, j: (0, 0, 0)),
            pl.BlockSpec((CHUNK, GROUPS), lambda i, j: (0, 0)),
            pl.BlockSpec((aw, tn), lambda i, j: (0, j)),
            pl.BlockSpec((tm, tn), lambda i, j: (i, j)),
            pl.BlockSpec((None, 1, tn), lambda i, j: (i // tpb, 0, j)),
        ],
        out_specs=pl.BlockSpec((tm, tn), lambda i, j: (i, j)),
        out_shape=jax.ShapeDtypeStruct((n, d), F32),
        scratch_shapes=[pltpu.VMEM((tm, aw), BF16), pltpu.VMEM((GROUPS, CHUNK, CHUNK), BF16)],
        compiler_params=_cparams("parallel", "arbitrary"),
        name="sgu_mix_out_proj",
    )(z, z, ln_g, ln_b, w_s, b_s_t, w_out, x2, gate)


def _ffn_kernel(x_ref, g_ref, sh_ref, sc_ref, gate_ref, wg_ref, wu_ref, wd_ref, fg_ref,
                o_ref, h_ref, *, final_norm, rc_edge, rc_mid):
    f = pl.program_id(1)
    last = pl.num_programs(1) - 1
    tm = x_ref.shape[0]

    def mlp(h):
        a = _mm(h, wg_ref[...])
        b = _mm(h, wu_ref[...])
        hid = (a * jax.nn.sigmoid(a) * b).astype(BF16)
        return gate_ref[...] * _mm(hid, wd_ref[...])

    @pl.when(f == 0)
    def _():
        gm = g_ref[...] * (1.0 + sc_ref[...])
        sh = sh_ref[...]
        for rs in _row_chunks(tm, rc_edge):
            x = x_ref[rs, :]
            h = _norm_mod(x, gm, sh).astype(BF16)
            h_ref[rs, :] = h
            o_ref[rs, :] = x + mlp(h)

    def accumulate(rc, normalize):
        for rs in _row_chunks(tm, rc):
            o = o_ref[rs, :] + mlp(h_ref[rs, :])
            if normalize:
                ms = jnp.mean(o * o, axis=-1, keepdims=True)
                o = o * lax.rsqrt(ms + EPS) * fg_ref[...]
            o_ref[rs, :] = o

    if final_norm:
        @pl.when((f > 0) & (f < last))
        def _():
            accumulate(rc_mid, False)

        @pl.when(f == last)
        def _():
            accumulate(rc_edge, True)
    else:
        @pl.when(f > 0)
        def _():
            accumulate(rc_mid, False)


def _ffn(x2, g, sh, sc, gate, wg, wu, wd, layer, final_g, seq, final_norm, tm=1024, tf=512,
         rc_edge=256, rc_mid=512):
    n, d = x2.shape
    fh = wg.shape[2]
    tpb = seq // tm
    vec = pl.BlockSpec((None, 1, d), lambda i, f: (i // tpb, 0, 0))
    return pl.pallas_call(
        functools.partial(_ffn_kernel, final_norm=final_norm, rc_edge=rc_edge, rc_mid=rc_mid),
        grid=(n // tm, fh // tf),
        in_specs=[
            pl.BlockSpec((tm, d), lambda i, f: (i, 0)),
            pl.BlockSpec((1, d), lambda i, f: (0, 0)),
            vec, vec, vec,
            pl.BlockSpec((None, d, tf), lambda i, f: (layer, 0, f)),
            pl.BlockSpec((None, d, tf), lambda i, f: (layer, 0, f)),
            pl.BlockSpec((None, tf, d), lambda i, f: (layer, f, 0)),
            pl.BlockSpec((1, d), lambda i, f: (0, 0)),
        ],
        out_specs=pl.BlockSpec((tm, d), lambda i, f: (i, 0)),
        out_shape=jax.ShapeDtypeStruct((n, d), F32),
        scratch_shapes=[pltpu.VMEM((tm, d), BF16)],
        compiler_params=_cparams("parallel", "arbitrary"),
        name="swiglu_ffn",
    )(x2, g, sh, sc, gate, wg, wu, wd, final_g)


def _attn_in_kernel(x_ref, g_ref, sh_ref, sc_ref, w_ref, wvt_ref, wf_ref, bf_ref,
                    o_ref, vt_ref, fc_ref, h_ref, carry_ref, *,
                    tiles_per_seq, q_tiles, qk_tiles, q_scale, rc, tq):
    i = pl.program_id(0)
    j = pl.program_id(1)
    tm = x_ref.shape[0]

    def emit(h, rs):
        z = _mm(h, w_ref[...]) * jnp.where(j < q_tiles, q_scale, 1.0)
        o_ref[rs, :] = z.astype(o_ref.dtype)

    @pl.when(j == 0)
    def _():
        @pl.when(i % tiles_per_seq == 0)
        def _():
            carry_ref[...] = jnp.zeros_like(carry_ref)

        gm = g_ref[...] * (1.0 + sc_ref[...])
        sh = sh_ref[...]
        row = lax.broadcasted_iota(jnp.int32, (rc, rc), 0)
        col = lax.broadcasted_iota(jnp.int32, (rc, rc), 1)
        tri = (col <= row).astype(BF16)
        for rs in _row_chunks(tm, rc):
            h = _norm_mod(x_ref[rs, :], gm, sh).astype(BF16)
            h_ref[rs, :] = h
            t = _mm(h, wf_ref[...]) + bf_ref[...]
            log_f = jnp.minimum(t, 0.0) - jnp.log1p(jnp.exp(-jnp.abs(t)))
            hi = log_f.astype(BF16)
            r1 = log_f - hi.astype(F32)
            mid = r1.astype(BF16)
            lo = (r1 - mid.astype(F32)).astype(BF16)
            c2 = _mm(tri, jnp.concatenate([hi, mid], axis=1))
            cum = c2[:, :LANES] + c2[:, LANES:] + _mm(tri, lo) + carry_ref[...]
            fc_ref[rs, :] = cum * LOG2E
            carry_ref[...] = cum[rc - 1:rc, :]
            emit(h, rs)

    @pl.when((j != 0) & (j < qk_tiles))
    def _():
        emit(h_ref[...], slice(None))

    @pl.when(j >= qk_tiles)
    def _():
        zt = lax.dot_general(wvt_ref[...], h_ref[...], (((1,), (1,)), ((), ())),
                             preferred_element_type=F32)
        for c in range(tm // tq):
            vt_ref[c] = zt[:, c * tq:(c + 1) * tq].astype(vt_ref.dtype)


def _attn_in(x2, g, sh, sc, w_all, layer, w_vt, w_f, b_f, seq, tq, tm=1024, tn=1024, rc=256):
    n, d = x2.shape
    tpb = seq // tm
    qk_tiles = 2 * d // tn
    v_tiles = d // tn
    kern = functools.partial(_attn_in_kernel, tiles_per_seq=tpb, q_tiles=d // tn,
                             qk_tiles=qk_tiles, q_scale=LOG2E / math.sqrt(HEAD_DIM), rc=rc, tq=tq)
    qk_col = lambda j: jnp.minimum(j, qk_tiles - 1)
    v_row = lambda j: jnp.maximum(j - qk_tiles, 0)
    return pl.pallas_call(
        kern,
        grid=(n // tm, qk_tiles + v_tiles),
        in_specs=[
            pl.BlockSpec((tm, d), lambda i, j: (i, 0)),
            pl.BlockSpec((1, d), lambda i, j: (0, 0)),
            pl.BlockSpec((None, 1, d), lambda i, j: (i // tpb, 0, 0)),
            pl.BlockSpec((None, 1, d), lambda i, j: (i // tpb, 0, 0)),
            pl.BlockSpec((None, d, tn), lambda i, j: (layer, 0, qk_col(j))),
            pl.BlockSpec((tn, d), lambda i, j: (v_row(j), 0)),
            pl.BlockSpec((d, LANES), lambda i, j: (0, 0)),
            pl.BlockSpec((1, LANES), lambda i, j: (0, 0)),
        ],
        out_specs=[
            pl.BlockSpec((tm, tn), lambda i, j: (i, qk_col(j))),
            pl.BlockSpec((tm // tq, tn, tq), lambda i, j: (i, v_row(j), 0)),
            pl.BlockSpec((tm, LANES), lambda i, j: (i, 0)),
        ],
        out_shape=[
            jax.ShapeDtypeStruct((n, 2 * d), BF16),
            jax.ShapeDtypeStruct((n // tq, d, tq), BF16),
            jax.ShapeDtypeStruct((n, LANES), F32),
        ],
        scratch_shapes=[pltpu.VMEM((tm, d), BF16), pltpu.VMEM((1, LANES), F32)],
        compiler_params=_cparams("arbitrary", "arbitrary"),
        name="attn_in_proj",
    )(x2, g, sh, sc, w_all, w_vt, w_f, b_f)


def _fox_kernel(q_ref, k_ref, vt_ref, fc_ref, fr_ref, o_ref, va_ref, ka_ref, *, tq, hb):
    hg = pl.program_id(1)
    qi = pl.program_id(2)
    dh = HEAD_DIM
    seq = k_ref.shape[0]
    da = dh + ONES_ROWS

    @pl.when(qi == 0)
    def _():
        lane = lax.broadcasted_iota(jnp.int32, (tq, LANES), 1)
        for c in range(seq // tq):
            rs = slice(c * tq, (c + 1) * tq)
            fc = fc_ref[rs, :]
            for hh in range(hb):
                col = jnp.sum(jnp.where(lane == hg * hb + hh, fc, 0.0), axis=1, keepdims=True)
                hi = col.astype(BF16).astype(F32)
                r1 = col - hi
                mid = r1.astype(BF16).astype(F32)
                ext = jnp.where(lane == 0, hi, jnp.where(lane == 1, mid,
                                                         jnp.where(lane == 2, r1 - mid, 0.0)))
                ka_ref[hh, rs, :dh] = k_ref[rs, hh * dh:(hh + 1) * dh]
                ka_ref[hh, rs, dh:] = ext.astype(BF16)
                va_ref[c, hh * da:hh * da + dh, :] = vt_ref[c, hh * dh:(hh + 1) * dh, :]
                va_ref[c, hh * da + dh:(hh + 1) * da, :] = jnp.ones((ONES_ROWS, tq), BF16)

    def step(c0, nck, carry, masked):
        tk = nck * tq
        start = pl.multiple_of(c0 * tq, tq)
        ts = []
        for hh in range(hb):
            ks = ka_ref[hh, pl.ds(start, tk), :]
            t = lax.dot_general(ks, qa[hh], (((1,), (1,)), ((), ())),
                                preferred_element_type=F32)
            if masked:
                kpos = lax.broadcasted_iota(jnp.int32, t.shape, 0)
                qpos = lax.broadcasted_iota(jnp.int32, t.shape, 1) + (nck - 1) * tq
                t = jnp.where(kpos <= qpos, t, -jnp.inf)
            ts.append(t)
        ps = []
        for hh, (m, acc) in enumerate(carry):
            fq = fr_ref[hh, pl.ds(qi, 1), :]
            m_new = jnp.maximum(m, jnp.max(ts[hh], axis=0, keepdims=True) + fq)
            p = jnp.exp2(ts[hh] + (fq - m_new))
            alpha = jnp.exp2(m - m_new)
            ps.append((m_new, alpha, p.astype(BF16)))
        out = []
        for hh, (m, acc) in enumerate(carry):
            m_new, alpha, p = ps[hh]
            va = jnp.concatenate([va_ref[c0 + c, hh * da:(hh + 1) * da, :] for c in range(nck)],
                                 axis=1)
            out.append((m_new, alpha * acc + _mm(va, p)))
        return tuple(out)

    lane_q = lax.broadcasted_iota(jnp.int32, (tq, LANES), 1)
    neg = jnp.where(lane_q < 3, -1.0, 0.0).astype(BF16)
    qa = [jnp.concatenate([q_ref[:, hh * dh:(hh + 1) * dh], neg], axis=1) for hh in range(hb)]

    def lazy_step(c0, nck, state):
        carry, ovf = state
        tk = nck * tq
        start = pl.multiple_of(c0 * tq, tq)
        ts = []
        for hh in range(hb):
            ks = ka_ref[hh, pl.ds(start, tk), :]
            ts.append(lax.dot_general(ks, qa[hh], (((1,), (1,)), ((), ())),
                                      preferred_element_type=F32))
        ps = []
        for hh, (m, acc) in enumerate(carry):
            fq = fr_ref[hh, pl.ds(qi, 1), :]
            p = jnp.exp2(ts[hh] + (fq - m)).astype(BF16)
            bmax = jnp.max(ts[hh], axis=0, keepdims=True) + fq
            m_new = jnp.maximum(m, bmax)
            ovf = jnp.maximum(ovf, bmax - m)
            ps.append((m_new, jnp.exp2(m - m_new), p))
        out = []
        for hh, (m, acc) in enumerate(carry):
            m_new, alpha, p = ps[hh]
            va = jnp.concatenate([va_ref[c0 + c, hh * da:(hh + 1) * da, :] for c in range(nck)],
                                 axis=1)
            out.append((m_new, alpha * (acc + _mm(va, p))))
        return tuple(out), ovf

    def finish(carry):
        for hh, (_, acc) in enumerate(carry):
            o_ref[:, hh * dh:(hh + 1) * dh] = (acc[:dh] / acc[dh:dh + 1]).T.astype(o_ref.dtype)

    init = tuple((jnp.full((1, tq), -jnp.inf, F32), jnp.zeros((da, tq), F32))
                 for _ in range(hb))
    state = (step(qi, 1, init, True), jnp.full((1, tq), -jnp.inf, F32))
    state = lax.fori_loop(0, qi // 2, lambda jj, st: lazy_step(2 * jj, 2, st), state)
    state = lax.cond(qi % 2 == 1, lambda st: lazy_step(qi - 1, 1, st), lambda st: st, state)
    carry, ovf = state
    safe = jnp.max(ovf) <= MAX_LAZY_EXP

    @pl.when(safe)
    def _():
        finish(carry)

    @pl.when(jnp.logical_not(safe))
    def _():
        c = lax.fori_loop(0, qi, lambda j, cc: step(j, 1, cc, False), init)
        finish(step(qi, 1, c, True))


def _fox_attention(qk, vt, fcum, frow, bsz, seq, tq=512, hb=4):
    n = qk.shape[0]
    nq = seq // tq
    ng = HEADS // hb
    w = hb * HEAD_DIM
    return pl.pallas_call(
        functools.partial(_fox_kernel, tq=tq, hb=hb),
        grid=(bsz, ng, nq),
        in_specs=[
            pl.BlockSpec((tq, w), lambda b, g, i: (b * nq + i, g)),
            pl.BlockSpec((seq, w), lambda b, g, i: (b, ng + g)),
            pl.BlockSpec((nq, w, tq), lambda b, g, i: (b, g, 0)),
            pl.BlockSpec((seq, LANES), lambda b, g, i: (b, 0)),
            pl.BlockSpec((hb, nq, tq), lambda b, g, i: (b * ng + g, 0, 0)),
        ],
        out_specs=pl.BlockSpec((tq, w), lambda b, g, i: (b * nq + i, g)),
        out_shape=jax.ShapeDtypeStruct((n, HEADS * HEAD_DIM), BF16),
        scratch_shapes=[pltpu.VMEM((nq, hb * (HEAD_DIM + ONES_ROWS), tq), BF16),
                        pltpu.VMEM((hb, seq, 2 * HEAD_DIM), BF16)],
        compiler_params=_cparams("arbitrary", "arbitrary", "arbitrary"),
        name="fox_attention",
    )(qk, qk, vt, fcum, frow)


def _proj_res_kernel(a_ref, w_ref, x_ref, gate_ref, o_ref):
    o_ref[...] = x_ref[...] + gate_ref[...] * _mm(a_ref[...], w_ref[...])


def _proj_res(a, w, x2, gate, seq, tm=512, tn=2048):
    n, d = x2.shape
    k = a.shape[1]
    tpb = seq // tm
    return pl.pallas_call(
        _proj_res_kernel,
        grid=(n // tm, d // tn),
        in_specs=[
            pl.BlockSpec((tm, k), lambda i, j: (i, 0)),
            pl.BlockSpec((k, tn), lambda i, j: (0, j)),
            pl.BlockSpec((tm, tn), lambda i, j: (i, j)),
            pl.BlockSpec((None, 1, tn), lambda i, j: (i // tpb, 0, j)),
        ],
        out_specs=pl.BlockSpec((tm, tn), lambda i, j: (i, j)),
        out_shape=jax.ShapeDtypeStruct((n, d), F32),
        compiler_params=_cparams("parallel", "arbitrary"),
        name="attn_out_proj",
    )(a, w, x2, gate)


def kernel(x, c, ada_w, ada_b, norm_mix_g, norm_ffn_g, a_w_in, a_b_in, a_ln_g, a_ln_b, a_w_s,
           a_b_s, a_w_out, b_w_in, b_b_f, b_w_out, ffn_w_gate, ffn_w_up, ffn_w_down, final_g):
    bsz, seq, d = x.shape
    depth = ada_w.shape[0]
    x2 = x.reshape(bsz * seq, d)

    mod = _ada_mod(c, ada_w, ada_b).reshape(depth, bsz, 6, 1, d)
    row = lambda a: a.reshape(1, -1)
    wg_bf16, wu_bf16, wd_bf16 = (w.astype(BF16) for w in (ffn_w_gate, ffn_w_up, ffn_w_down))
    b_w_in_bf16 = b_w_in.astype(BF16)

    for i in range(depth):
        sh1, sc1, g1, sh2, sc2, g2 = (mod[i, :, k] for k in range(6))
        j = i // 2
        if i % 2 == 0:
            z = _sgu_in(x2, row(norm_mix_g[i]), sh1, sc1, a_w_in[j].astype(BF16),
                        row(a_b_in[j]), seq)
            x2 = _sgu_out(z, row(a_ln_g[j]), row(a_ln_b[j]), a_w_s[j], a_b_s[j].T,
                          a_w_out[j].astype(BF16), x2, g1, seq)
        else:
            tq = 512
            w_in = b_w_in[j]
            w_vt = w_in[:, 2 * d:3 * d].T.astype(BF16)
            w_f = jnp.pad(w_in[:, 3 * d:], ((0, 0), (0, LANES - HEADS))).astype(BF16)
            b_f = jnp.pad(b_b_f[j], (0, LANES - HEADS)).reshape(1, LANES)
            qk, vt, fcum = _attn_in(x2, row(norm_mix_g[i]), sh1, sc1, b_w_in_bf16, j,
                                    w_vt, w_f, b_f, seq, tq)
            frow = fcum[:, :HEADS].reshape(bsz, seq, HEADS).transpose(0, 2, 1)
            frow = frow.reshape(bsz * HEADS, seq // tq, tq)
            o = _fox_attention(qk, vt, fcum, frow, bsz, seq, tq=tq)
            x2 = _proj_res(o, b_w_out[j].astype(BF16), x2, g1, seq)
        x2 = _ffn(x2, row(norm_ffn_g[i]), sh2, sc2, g2, wg_bf16, wu_bf16, wd_bf16, i,
                  row(final_g), seq, final_norm=(i == depth - 1))
    return x2.reshape(bsz, seq, d)
```
